```python
import jax
import jax.numpy as jnp
from jax import lax
import numpy as np


D_MODEL = 2048
BATCH = 2
SEQ = 16384
DEPTH = 2

CHUNK = 64
D_MIX = D_MODEL
N_GROUPS = 4
D_GROUP = D_MIX // N_GROUPS
POOL_WINDOWS = (2, 4, 8, 16)
POOL_CH = D_GROUP // len(POOL_WINDOWS)
SGU_BLOCK = 128
SGU_HEADS = 4
SGU_HEAD_DIM = D_GROUP // SGU_HEADS
CONV_WIDTH = 31
HGRN_HEADS = 4
HGRN_HEAD_DIM = D_GROUP // HGRN_HEADS
FORGET_FLOOR = 1e-20
D_FF = 4 * D_MODEL
EPS = 1e-6
IN_A = D_GROUP
IN_B = 2 * D_GROUP
IN_C = 2 * D_GROUP
IN_D = 4 * D_GROUP
D_IN = IN_A + IN_B + IN_C + IN_D
N_MOD = 6

kernel_name = 'hybrid_pool_sgu_conv_hgrn2_encoder'


def rms_norm(x, g):
    xf = x.astype(jnp.float32)
    y = xf * lax.rsqrt(jnp.mean(xf * xf, axis=-1, keepdims=True) + EPS)
    return (y * g.astype(jnp.float32)).astype(x.dtype)


def layer_norm(x, g, b):
    xf = x.astype(jnp.float32)
    mu = jnp.mean(xf, axis=-1, keepdims=True)
    var = jnp.mean(jnp.square(xf - mu), axis=-1, keepdims=True)
    y = (xf - mu) * lax.rsqrt(var + EPS)
    return (y * g.astype(jnp.float32) + b.astype(jnp.float32)).astype(x.dtype)


def pool_mixer(xa, w, scale):
    B, T, _ = xa.shape
    xf = xa.astype(jnp.float32).reshape(B, T, len(POOL_WINDOWS), POOL_CH)
    cs = jnp.cumsum(xf, axis=1)
    t = jnp.arange(T)
    outs = []
    for gi, win in enumerate(POOL_WINDOWS):
        c = cs[:, :, gi]
        lag = jnp.pad(c, ((0, 0), (win, 0), (0, 0)))[:, :T]
        cnt = jnp.minimum(t + 1, win).astype(jnp.float32)[None, :, None]
        outs.append((c - lag) / cnt)
    pooled = (jnp.stack(outs, axis=2) - xf).astype(xa.dtype)
    y = jnp.einsum('btgc,gcd->btgd', pooled, w)
    return y.reshape(B, T, D_GROUP) * scale


def sgu_mixer(xb, norm_g, norm_b, ws, bs):
    B, T, _ = xb.shape
    u, v = jnp.split(xb, 2, axis=-1)
    v = layer_norm(v, norm_g, norm_b)
    n = T // SGU_BLOCK
    v = v.reshape(B, n, SGU_BLOCK, SGU_HEADS, SGU_HEAD_DIM)
    pos = jnp.arange(SGU_BLOCK) // CHUNK
    mask = pos[None, :] <= pos[:, None]
    wm = jnp.where(mask[None], ws, jnp.zeros_like(ws))
    mixed = jnp.einsum('hij,bnjhd->bnihd', wm, v) + bs.T[None, None, :, :, None]
    return u * mixed.reshape(B, T, D_GROUP)


def conv_module(xc, dw, dw_b, ng, nb, pw, pw_b):
    a, g = jnp.split(xc, 2, axis=-1)
    h = a * jax.nn.sigmoid(g)
    h = lax.conv_general_dilated(
        h, dw[:, None, :], window_strides=(1,), padding=[(CONV_WIDTH - 1, 0)],
        dimension_numbers=('NWC', 'WIO', 'NWC'), feature_group_count=D_GROUP) + dw_b
    h = jax.nn.silu(layer_norm(h, ng, nb))
    return h @ pw + pw_b


def hgrn_lower_bounds(logits):
    sm = jax.nn.softmax(logits.astype(jnp.float32), axis=0)
    return jnp.cumsum(sm, axis=0) - sm[0:1]


def hgrn2_mixer(xd, lb, norm_g):
    B, T, _ = xd.shape
    q, fz, iv, og = jnp.split(xd, 4, axis=-1)
    fz = fz.astype(jnp.float32)
    f = lb + (1.0 - lb) * jax.nn.sigmoid(fz)
    logf = jnp.log(jnp.maximum(f, FORGET_FLOOR))
    k = (1.0 - lb) * jax.nn.sigmoid(-fz)
    n = T // CHUNK

    def to_chunks(a):
        a = a.astype(jnp.float32).reshape(B, n, CHUNK, HGRN_HEADS, HGRN_HEAD_DIM)
        return jnp.transpose(a, (1, 0, 3, 2, 4))

    qc = to_chunks(q) * (HGRN_HEAD_DIM ** -0.5)
    kc = to_chunks(k)
    vc = to_chunks(iv)
    bc = jnp.cumsum(to_chunks(logf), axis=3)
    tri = jnp.tril(jnp.ones((CHUNK, CHUNK), dtype=bool))[None, None, :, :, None]

    def step(S, inp):
        qq, kk, vv, bb = inp
        inter = jnp.einsum('bhck,bhkv->bhcv', qq * jnp.exp(bb), S)
        diff = bb[:, :, :, None, :] - bb[:, :, None, :, :]
        decay = jnp.where(tri, jnp.exp(jnp.minimum(diff, 0.0)), 0.0)
        att = jnp.einsum('bhik,bhjk,bhijk->bhij', qq, kk, decay)
        intra = jnp.einsum('bhij,bhjv->bhiv', att, vv)
        b_last = bb[:, :, -1]
        kd = kk * jnp.exp(jnp.minimum(b_last[:, :, None, :] - bb, 0.0))
        S = jnp.exp(b_last)[..., None] * S + jnp.einsum('bhck,bhcv->bhkv', kd, vv)
        return S, inter + intra

    S0 = jnp.zeros((B, HGRN_HEADS, HGRN_HEAD_DIM, HGRN_HEAD_DIM), jnp.float32)
    _, o = lax.scan(step, S0, (qc, kc, vc, bc))
    o = jnp.transpose(o, (1, 0, 3, 2, 4)).reshape(B, T, HGRN_HEADS, HGRN_HEAD_DIM)
    o = o * lax.rsqrt(jnp.mean(o * o, axis=-1, keepdims=True) + EPS)
    o = o * norm_g.astype(jnp.float32).reshape(HGRN_HEADS, HGRN_HEAD_DIM)
    return o.reshape(B, T, D_GROUP).astype(xd.dtype) * jax.nn.silu(og)


def setup_inputs(seed: int = 0) -> dict:
    key = jax.random.key(seed)
    ks = jax.random.split(key, 32)
    L, D = DEPTH, D_MODEL
    nrm = lambda k, shape, s: jax.random.normal(k, shape, jnp.float32) * s
    gain = lambda k, shape: 1.0 + nrm(k, shape, 0.05)
    return {
        'x': nrm(ks[0], (BATCH, SEQ, D), 1.0),
        'c': nrm(ks[1], (BATCH, D), 1.0),
        'ada_w': nrm(ks[2], (L, D, N_MOD * D), 0.5 * D ** -0.5),
        'ada_b': nrm(ks[3], (L, N_MOD * D), 0.01),
        'norm_mix_pre': gain(ks[4], (L, D)),
        'norm_mix_post': gain(ks[5], (L, D)),
        'norm_mlp_pre': gain(ks[6], (L, D)),
        'norm_mlp_post': gain(ks[7], (L, D)),
        'w_in': nrm(ks[8], (L, D, D_IN), D ** -0.5),
        'pool_w': nrm(ks[9], (L, len(POOL_WINDOWS), POOL_CH, POOL_CH), POOL_CH ** -0.5),
        'pool_scale': 1.0 + nrm(ks[10], (L, D_GROUP), 0.1),
        'sgu_norm_g': gain(ks[11], (L, D_GROUP)),
        'sgu_norm_b': nrm(ks[12], (L, D_GROUP), 0.01),
        'sgu_w': nrm(ks[13], (L, SGU_HEADS, SGU_BLOCK, SGU_BLOCK), SGU_BLOCK ** -0.5),
        'sgu_b': 1.0 + nrm(ks[14], (L, SGU_HEADS, SGU_BLOCK), 0.05),
        'conv_dw': nrm(ks[15], (L, CONV_WIDTH, D_GROUP), CONV_WIDTH ** -0.5),
        'conv_dw_b': nrm(ks[16], (L, D_GROUP), 0.01),
        'conv_norm_g': gain(ks[17], (L, D_GROUP)),
        'conv_norm_b': nrm(ks[18], (L, D_GROUP), 0.01),
        'conv_pw': nrm(ks[19], (L, D_GROUP, D_GROUP), D_GROUP ** -0.5),
        'conv_pw_b': nrm(ks[20], (L, D_GROUP), 0.01),
        'hgrn_lb_logits': nrm(ks[21], (L, D_GROUP), 0.5),
        'hgrn_norm_g': gain(ks[22], (L, D_GROUP)),
        'w_out': nrm(ks[23], (L, D_MIX, D), D_MIX ** -0.5),
        'mlp_w1': nrm(ks[24], (L, D, D_FF), D ** -0.5),
        'mlp_w2': nrm(ks[25], (L, D_FF, D), D_FF ** -0.5),
    }


def reference(x, c, ada_w, ada_b, norm_mix_pre, norm_mix_post, norm_mlp_pre, norm_mlp_post,
              w_in, pool_w, pool_scale, sgu_norm_g, sgu_norm_b, sgu_w, sgu_b,
              conv_dw, conv_dw_b, conv_norm_g, conv_norm_b, conv_pw, conv_pw_b,
              hgrn_lb_logits, hgrn_norm_g, w_out, mlp_w1, mlp_w2):
    lb_all = hgrn_lower_bounds(hgrn_lb_logits)
    cond = jax.nn.silu(c)
    for l in range(DEPTH):
        mod = (cond @ ada_w[l] + ada_b[l])[:, None, :]
        sh_m, sc_m, gt_m, sh_f, sc_f, gt_f = jnp.split(mod, N_MOD, axis=-1)
        h = rms_norm(x, norm_mix_pre[l]) * (1.0 + sc_m) + sh_m
        z = h @ w_in[l]
        za, zb, zc, zd = jnp.split(z, [IN_A, IN_A + IN_B, IN_A + IN_B + IN_C], axis=-1)
        ya = pool_mixer(za, pool_w[l], pool_scale[l])
        yb = sgu_mixer(zb, sgu_norm_g[l], sgu_norm_b[l], sgu_w[l], sgu_b[l])
        yc = conv_module(zc, conv_dw[l], conv_dw_b[l], conv_norm_g[l], conv_norm_b[l],
                         conv_pw[l], conv_pw_b[l])
        yd = hgrn2_mixer(zd, lb_all[l], hgrn_norm_g[l])
        y = jnp.concatenate([ya, yb, yc, yd], axis=-1) @ w_out[l]
        x = x + gt_m * rms_norm(y, norm_mix_post[l])
        h = rms_norm(x, norm_mlp_pre[l]) * (1.0 + sc_f) + sh_f
        y = jnp.square(jax.nn.relu(h @ mlp_w1[l])) @ mlp_w2[l]
        x = x + gt_f * rms_norm(y, norm_mlp_post[l])
    return x
```

```python
import functools

import numpy as np
import jax
import jax.numpy as jnp
from jax import lax
from jax.experimental import pallas as pl
from jax.experimental.pallas import tpu as pltpu

F32 = jnp.float32
BF16 = jnp.bfloat16

EPS = 1e-6
N_MOD = 6
N_GROUPS = 4
POOL_WINDOWS = (2, 4, 8, 16)
POOL_HALO = 16
SGU_BLOCK = 128
CHUNK = 64
CONV_WIDTH = 31
CONV_HALO = 32
HEAD_DIM = 128
FORGET_FLOOR = 1e-20
N_LEVELS = 7

VMEM_LIMIT = 56 * 1024 * 1024


def _cparams(sem):
    return pltpu.CompilerParams(dimension_semantics=sem, vmem_limit_bytes=VMEM_LIMIT)


def _rms(x, g):
    return x * lax.rsqrt(jnp.mean(x * x, axis=-1, keepdims=True) + EPS) * g


def _layer_norm(x, g, b):
    mu = jnp.mean(x, axis=-1, keepdims=True)
    xc = x - mu
    var = jnp.mean(xc * xc, axis=-1, keepdims=True)
    return xc * lax.rsqrt(var + EPS) * g + b


def _dot(a, b):
    return jnp.dot(a, b, preferred_element_type=F32)


def _dot_nt(a, b):
    return lax.dot_general(a, b, (((1,), (1,)), ((), ())), preferred_element_type=F32)


def _dot_tn(a, b):
    return lax.dot_general(a, b, (((0,), (0,)), ((), ())), preferred_element_type=F32)


def _mod_kernel(c_ref, w_ref, b_ref, o_ref):
    c = c_ref[...]
    cond = c * jax.nn.sigmoid(c)
    o_ref[0] = jnp.dot(cond, w_ref[0], preferred_element_type=F32,
                       precision=lax.Precision.HIGHEST) + b_ref[0]


def _modulation(c, ada_w, ada_b):
    B, D = c.shape
    L, _, N = ada_w.shape
    bp = 8
    tn = 1024
    cp = jnp.zeros((bp, D), F32).at[:B].set(c)
    out = pl.pallas_call(
        _mod_kernel,
        grid=(L, N // tn),
        in_specs=[
            pl.BlockSpec((bp, D), lambda l, j: (0, 0)),
            pl.BlockSpec((1, D, tn), lambda l, j: (l, 0, j)),
            pl.BlockSpec((1, 1, tn), lambda l, j: (l, 0, j)),
        ],
        out_specs=pl.BlockSpec((1, bp, tn), lambda l, j: (l, 0, j)),
        out_shape=jax.ShapeDtypeStruct((L, bp, N), F32),
        compiler_params=_cparams(("arbitrary", "arbitrary")),
        name="ada_mod",
    )(cp, ada_w, ada_b.reshape(L, 1, N))
    return out[:, :B].reshape(L, B, N_MOD, D)


def _in_proj_kernel(x_ref, mod_ref, g_ref, w_ref, z_ref, h_scr):
    @pl.when(pl.program_id(1) == 0)
    def _():
        h = _rms(x_ref[...], g_ref[...]) * (1.0 + mod_ref[0, 1:2, :]) + mod_ref[0, 0:1, :]
        h_scr[...] = h.astype(BF16)

    z_ref[...] = _dot(h_scr[...], w_ref[...])


def _in_proj(x2, mod, g, w, T, tm=512, tn=1536):
    M, D = x2.shape
    N = w.shape[1]
    tpb = T // tm
    return pl.pallas_call(
        _in_proj_kernel,
        grid=(M // tm, N // tn),
        in_specs=[
            pl.BlockSpec((tm, D), lambda i, j: (i, 0)),
            pl.BlockSpec((1, N_MOD, D), lambda i, j: (i // tpb, 0, 0)),
            pl.BlockSpec((1, D), lambda i, j: (0, 0)),
            pl.BlockSpec((D, tn), lambda i, j: (0, j)),
        ],
        out_specs=pl.BlockSpec((tm, tn), lambda i, j: (i, j)),
        out_shape=jax.ShapeDtypeStruct((M, N), F32),
        scratch_shapes=[pltpu.VMEM((tm, D), BF16)],
        compiler_params=_cparams(("arbitrary", "arbitrary")),
        name="in_proj",
    )(x2, mod, g, w)


def _pool_kernel(x_ref, halo_ref, w_ref, s_ref, o_ref, buf):
    i = pl.program_id(1)
    tt = x_ref.shape[1]
    pc = w_ref.shape[1]
    buf[POOL_HALO:POOL_HALO + tt, :] = x_ref[0]
    buf[0:POOL_HALO, :] = jnp.where(i == 0, 0.0, halo_ref[0])
    t = i * tt + lax.broadcasted_iota(jnp.int32, (tt, pc), 0)
    for gi, win in enumerate(POOL_WINDOWS):
        cols = slice(gi * pc, (gi + 1) * pc)
        x = buf[POOL_HALO:POOL_HALO + tt, cols]
        acc = x
        for k in range(1, win):
            acc = acc + buf[POOL_HALO - k:POOL_HALO - k + tt, cols]
        cnt = jnp.minimum(t + 1, win).astype(F32)
        pooled = acc / cnt - x
        y = _dot(pooled.astype(BF16), w_ref[gi]) * s_ref[:, cols]
        o_ref[0, :, cols] = y.astype(o_ref.dtype)


def _pool_mixer(z3, w, scale, tt=512):
    B, T, _ = z3.shape
    dg = scale.shape[-1]
    hb = tt // POOL_HALO
    return pl.pallas_call(
        _pool_kernel,
        grid=(B, T // tt),
        in_specs=[
            pl.BlockSpec((1, tt, dg), lambda b, i: (b, i, 0)),
            pl.BlockSpec((1, POOL_HALO, dg), lambda b, i: (b, jnp.maximum(i * hb - 1, 0), 0)),
            pl.BlockSpec(w.shape, lambda b, i: (0, 0, 0)),
            pl.BlockSpec((1, dg), lambda b, i: (0, 0)),
        ],
        out_specs=pl.BlockSpec((1, tt, dg), lambda b, i: (b, i, 0)),
        out_shape=jax.ShapeDtypeStruct((B, T, dg), BF16),
        scratch_shapes=[pltpu.VMEM((tt + POOL_HALO, dg), F32)],
        compiler_params=_cparams(("arbitrary", "arbitrary")),
        name="pool_mixer",
    )(z3, z3, w, scale)


def _sgu_kernel(u_ref, v_ref, g_ref, b_ref, w_ref, bst_ref, o_ref):
    tt = u_ref.shape[1]
    nh = w_ref.shape[0]
    hd = u_ref.shape[2] // nh
    vb = _layer_norm(v_ref[0], g_ref[...], b_ref[...]).astype(BF16)
    r = lax.broadcasted_iota(jnp.int32, (SGU_BLOCK, SGU_BLOCK), 0)
    c = lax.broadcasted_iota(jnp.int32, (SGU_BLOCK, SGU_BLOCK), 1)
    causal = (c // CHUNK) <= (r // CHUNK)
    for h in range(nh):
        cols = slice(h * hd, (h + 1) * hd)
        wm = jnp.where(causal, w_ref[h], 0.0).astype(BF16)
        bias = bst_ref[:, h:h + 1]
        for n in range(tt // SGU_BLOCK):
            rows = slice(n * SGU_BLOCK, (n + 1) * SGU_BLOCK)
            mixed = _dot(wm, vb[rows, cols]) + bias
            o_ref[0, rows, cols] = (u_ref[0, rows, cols] * mixed).astype(o_ref.dtype)


def _sgu_mixer(z3, norm_g, norm_b, ws, bs, tt=512):
    B, T, _ = z3.shape
    dg = norm_g.shape[-1]
    nh = ws.shape[0]
    return pl.pallas_call(
        _sgu_kernel,
        grid=(B, T // tt),
        in_specs=[
            pl.BlockSpec((1, tt, dg), lambda b, i: (b, i, 1)),
            pl.BlockSpec((1, tt, dg), lambda b, i: (b, i, 2)),
            pl.BlockSpec((1, dg), lambda b, i: (0, 0)),
            pl.BlockSpec((1, dg), lambda b, i: (0, 0)),
            pl.BlockSpec(ws.shape, lambda b, i: (0, 0, 0)),
            pl.BlockSpec((SGU_BLOCK, nh), lambda b, i: (0, 0)),
        ],
        out_specs=pl.BlockSpec((1, tt, dg), lambda b, i: (b, i, 0)),
        out_shape=jax.ShapeDtypeStruct((B, T, dg), BF16),
        compiler_params=_cparams(("arbitrary", "arbitrary")),
        name="sgu_mixer",
    )(z3, z3, norm_g, norm_b, ws, bs.T)


CONV_ROWS = 32


def _conv_kernel(a_ref, g_ref, ah_ref, gh_ref, dw_ref, dwb_ref, ng_ref, nb_ref, pw_ref, pwb_ref,
                 o_ref, buf, cbuf):
    i = pl.program_id(1)
    tt = a_ref.shape[1]
    buf[CONV_HALO:CONV_HALO + tt, :] = a_ref[0] * jax.nn.sigmoid(g_ref[0])
    buf[0:CONV_HALO, :] = jnp.where(i == 0, 0.0, ah_ref[0] * jax.nn.sigmoid(gh_ref[0]))
    base = CONV_HALO - (CONV_WIDTH - 1)
    for r in range(tt // CONV_ROWS):
        r0 = r * CONV_ROWS
        acc = buf[base + r0:base + r0 + CONV_ROWS, :] * dw_ref[0:1, :] + dwb_ref[...]
        for k in range(1, CONV_WIDTH):
            acc = acc + buf[base + r0 + k:base + r0 + k + CONV_ROWS, :] * dw_ref[k:k + 1, :]
        cbuf[r0:r0 + CONV_ROWS, :] = acc
    h = _layer_norm(cbuf[...], ng_ref[...], nb_ref[...])
    h = h * jax.nn.sigmoid(h)
    o_ref[0] = (_dot(h.astype(BF16), pw_ref[...]) + pwb_ref[...]).astype(o_ref.dtype)


def _conv_mixer(z3, dw, dw_b, ng, nb, pw, pw_b, tt=256):
    B, T, _ = z3.shape
    dg = dw.shape[-1]
    hb = tt // CONV_HALO
    halo_map = lambda col: (lambda b, i: (b, jnp.maximum(i * hb - 1, 0), col))
    vec = pl.BlockSpec((1, dg), lambda b, i: (0, 0))
    return pl.pallas_call(
        _conv_kernel,
        grid=(B, T // tt),
        in_specs=[
            pl.BlockSpec((1, tt, dg), lambda b, i: (b, i, 3)),
            pl.BlockSpec((1, tt, dg), lambda b, i: (b, i, 4)),
            pl.BlockSpec((1, CONV_HALO, dg), halo_map(3)),
            pl.BlockSpec((1, CONV_HALO, dg), halo_map(4)),
            pl.BlockSpec(dw.shape, lambda b, i: (0, 0)),
            vec, vec, vec,
            pl.BlockSpec(pw.shape, lambda b, i: (0, 0)),
            vec,
        ],
        out_specs=pl.BlockSpec((1, tt, dg), lambda b, i: (b, i, 0)),
        out_shape=jax.ShapeDtypeStruct((B, T, dg), BF16),
        scratch_shapes=[pltpu.VMEM((tt + CONV_HALO, dg), F32), pltpu.VMEM((tt, dg), F32)],
        compiler_params=_cparams(("arbitrary", "arbitrary")),
        name="conv_mixer",
    )(z3, z3, z3, z3, dw, dw_b, ng, nb, pw, pw_b)


def _hgrn_constants():
    C = CHUNK
    r = np.arange(C)
    blocks = [(r[None, :] <= r[:, None]), (r[None, :] > r[:, None])]
    lvl = np.full((C, C), N_LEVELS, np.int32)
    lvl[r, r] = 0
    for l in range(1, N_LEVELS):
        s = C >> l
        p = r % (2 * s)
        m = r - p + s - 1
        upper = p >= s
        up = (r[None, :] > m[:, None]) & (r[None, :] <= r[:, None])
        lo = (r[None, :] > r[:, None]) & (r[None, :] <= m[:, None])
        blocks.append(np.where(upper[:, None], up, lo))
        same = (r[:, None] // (2 * s)) == (r[None, :] // (2 * s))
        lvl[same & upper[:, None] & (~upper)[None, :]] = l
    mat = np.concatenate(blocks, axis=0).astype(np.float32)
    return np.concatenate([mat, mat], axis=1), lvl


def _hgrn_kernel(q_ref, f_ref, v_ref, og_ref, lg_ref, ng_ref, mc_ref, lvl_ref, o_ref, s_scr, o_scr,
                 *, layer):
    i = pl.program_id(2)
    tt = q_ref.shape[1]

    @pl.when(i == 0)
    def _():
        s_scr[...] = jnp.zeros_like(s_scr)

    depth = lg_ref.shape[0]
    rows = [lg_ref[m:m + 1, :] for m in range(depth)]
    mx = functools.reduce(jnp.maximum, rows)
    es = [jnp.exp(row - mx) for row in rows]
    den = functools.reduce(lambda a, b: a + b, es)
    sm = [e / den for e in es]
    lb = functools.reduce(lambda a, b: a + b, sm[:layer + 1]) - sm[0]

    fz = f_ref[0]
    f = lb + (1.0 - lb) * jax.nn.sigmoid(fz)
    logf = jnp.log(jnp.maximum(f, FORGET_FLOOR))
    kk = (1.0 - lb) * jax.nn.sigmoid(-fz)
    lvl = lvl_ref[...]
    scale = HEAD_DIM ** -0.5

    st = s_scr[...]
    for c in range(tt // CHUNK):
        rows = slice(c * CHUNK, (c + 1) * CHUNK)
        lf = logf[rows]
        hi = lf.astype(BF16)
        lo = (lf - hi.astype(F32)).astype(BF16)
        e_all = _dot(mc_ref[...], jnp.concatenate([hi, lo], axis=0))
        bcum = e_all[0:CHUNK]
        qc = q_ref[0, rows, :] * scale
        kc = kk[rows]
        vc = v_ref[0, rows, :].astype(BF16)
        inter = _dot_nt((qc * jnp.exp(bcum)).astype(BF16), st.astype(BF16))
        att = _dot_nt(qc.astype(BF16), kc.astype(BF16))
        att = jnp.where(lvl == 0, att, 0.0)
        for l in range(1, N_LEVELS):
            ex = jnp.exp(e_all[(l + 1) * CHUNK:(l + 2) * CHUNK])
            a = _dot_nt((qc * ex).astype(BF16), (kc * ex).astype(BF16))
            att = jnp.where(lvl == l, a, att)
        o_scr[rows, :] = inter + _dot(att.astype(BF16), vc)
        kd = (kc * jnp.exp(e_all[CHUNK:2 * CHUNK])).astype(BF16)
        st = st * jnp.exp(bcum[CHUNK - 1:CHUNK, :]) + _dot_tn(vc, kd)
    s_scr[...] = st

    o = _rms(o_scr[...], ng_ref[...])
    og = og_ref[0]
    o_ref[0] = (o * (og * jax.nn.sigmoid(og))).astype(o_ref.dtype)


def _hgrn_mixer(z3, lb_logits, norm_g, layer, col0, tt=512):
    B, T, _ = z3.shape
    dg = norm_g.shape[-1]
    nh = dg // HEAD_DIM
    c0 = col0 // HEAD_DIM
    mat, lvl = _hgrn_constants()
    part = lambda p: pl.BlockSpec((1, tt, HEAD_DIM), lambda b, h, i: (b, i, c0 + p * nh + h))
    return pl.pallas_call(
        functools.partial(_hgrn_kernel, layer=layer),
        grid=(B, nh, T // tt),
        in_specs=[
            part(0), part(1), part(2), part(3),
            pl.BlockSpec((lb_logits.shape[0], HEAD_DIM), lambda b, h, i: (0, h)),
            pl.BlockSpec((1, HEAD_DIM), lambda b, h, i: (0, h)),
            pl.BlockSpec(mat.shape, lambda b, h, i: (0, 0)),
            pl.BlockSpec(lvl.shape, lambda b, h, i: (0, 0)),
        ],
        out_specs=pl.BlockSpec((1, tt, HEAD_DIM), lambda b, h, i: (b, i, h)),
        out_shape=jax.ShapeDtypeStruct((B, T, dg), BF16),
        scratch_shapes=[pltpu.VMEM((HEAD_DIM, HEAD_DIM), F32), pltpu.VMEM((tt, HEAD_DIM), F32)],
        compiler_params=_cparams(("arbitrary", "arbitrary", "arbitrary")),
        name="hgrn_mixer",
    )(z3, z3, z3, z3, lb_logits, norm_g, jnp.asarray(mat, BF16), jnp.asarray(lvl))


OUT_ROWS = 256


def _out_proj_kernel(ya_ref, yb_ref, yc_ref, yd_ref, w_ref, x_ref, mod_ref, gpost_ref, gpre_ref,
                     x1_ref, h2_ref, cat):
    tm = x_ref.shape[0]
    dg = ya_ref.shape[1]
    for p, ref in enumerate((ya_ref, yb_ref, yc_ref, yd_ref)):
        cat[:, p * dg:(p + 1) * dg] = ref[...]
    for r in range(tm // OUT_ROWS):
        rows = slice(r * OUT_ROWS, (r + 1) * OUT_ROWS)
        y = _dot(cat[rows, :], w_ref[...])
        x1 = x_ref[rows, :] + mod_ref[0, 2:3, :] * _rms(y, gpost_ref[...])
        x1_ref[rows, :] = x1
        h2 = _rms(x1, gpre_ref[...]) * (1.0 + mod_ref[0, 4:5, :]) + mod_ref[0, 3:4, :]
        h2_ref[rows, :] = h2.astype(h2_ref.dtype)


def _out_proj(ys, w, x2, mod, g_post, g_pre, T, tm=512):
    M, D = x2.shape
    dg = ys[0].shape[1]
    tpb = T // tm
    yspec = pl.BlockSpec((tm, dg), lambda i: (i, 0))
    vec = pl.BlockSpec((1, D), lambda i: (0, 0))
    return pl.pallas_call(
        _out_proj_kernel,
        grid=(M // tm,),
        in_specs=[
            yspec, yspec, yspec, yspec,
            pl.BlockSpec(w.shape, lambda i: (0, 0)),
            pl.BlockSpec((tm, D), lambda i: (i, 0)),
            pl.BlockSpec((1, N_MOD, D), lambda i: (i // tpb, 0, 0)),
            vec, vec,
        ],
        out_specs=[pl.BlockSpec((tm, D), lambda i: (i, 0)), pl.BlockSpec((tm, D), lambda i: (i, 0))],
        out_shape=[jax.ShapeDtypeStruct((M, D), F32), jax.ShapeDtypeStruct((M, D), BF16)],
        scratch_shapes=[pltpu.VMEM((tm, len(ys) * dg), BF16)],
        compiler_params=_cparams(("arbitrary",)),
        name="out_proj",
    )(*ys, w, x2, mod, g_post, g_pre)


def _mlp_kernel(h_ref, w1_ref, w2_ref, x1_ref, mod_ref, g_ref, o_ref):
    j = pl.program_id(1)
    a = _dot(h_ref[...], w1_ref[...])
    a = jnp.square(jnp.maximum(a, 0.0)).astype(BF16)
    p = _dot(a, w2_ref[...])

    @pl.when(j == 0)
    def _():
        o_ref[...] = p

    @pl.when(j > 0)
    def _():
        o_ref[...] += p

    @pl.when(j == pl.num_programs(1) - 1)
    def _():
        o_ref[...] = x1_ref[...] + mod_ref[0, 5:6, :] * _rms(o_ref[...], g_ref[...])


def _mlp(h2, w1, w2, x1, mod, g, T, tm=512, tf=512):
    M, D = x1.shape
    FF = w1.shape[1]
    tpb = T // tm
    return pl.pallas_call(
        _mlp_kernel,
        grid=(M // tm, FF // tf),
        in_specs=[
            pl.BlockSpec((tm, D), lambda i, j: (i, 0)),
            pl.BlockSpec((D, tf), lambda i, j: (0, j)),
            pl.BlockSpec((tf, D), lambda i, j: (j, 0)),
            pl.BlockSpec((tm, D), lambda i, j: (i, 0)),
            pl.BlockSpec((1, N_MOD, D), lambda i, j: (i // tpb, 0, 0)),
            pl.BlockSpec((1, D), lambda i, j: (0, 0)),
        ],
        out_specs=pl.BlockSpec((tm, D), lambda i, j: (i, 0)),
        out_shape=jax.ShapeDtypeStruct((M, D), F32),
        compiler_params=_cparams(("arbitrary", "arbitrary")),
        name="mlp",
    )(h2, w1, w2, x1, mod, g)


def kernel(x, c, ada_w, ada_b, norm_mix_pre, norm_mix_post, norm_mlp_pre, norm_mlp_post, w_in, pool_w, pool_scale, sgu_norm_g, sgu_norm_b, sgu_w, sgu_b, conv_dw, conv_dw_b, conv_norm_g, conv_norm_b, conv_pw, conv_pw_b, hgrn_lb_logits, hgrn_norm_g, w_out, mlp_w1, mlp_w2):
    B, T, D = x.shape
    L = ada_w.shape[0]
    dg = D // N_GROUPS
    row = lambda a: a.reshape(1, -1)

    mod_all = _modulation(c, ada_w, ada_b)
    x2 = x.reshape(B * T, D)
    for l in range(L):
        mod = mod_all[l]
        z = _in_proj(x2, mod, row(norm_mix_pre[l]), w_in[l].astype(BF16), T)
        z3 = z.reshape(B, T, -1)
        ya = _pool_mixer(z3, pool_w[l].astype(BF16), row(pool_scale[l]))
        yb = _sgu_mixer(z3, row(sgu_norm_g[l]), row(sgu_norm_b[l]), sgu_w[l], sgu_b[l])
        yc = _conv_mixer(z3, conv_dw[l], row(conv_dw_b[l]), row(conv_norm_g[l]), row(conv_norm_b[l]),
                         conv_pw[l].astype(BF16), row(conv_pw_b[l]))
        yd = _hgrn_mixer(z3, hgrn_lb_logits, row(hgrn_norm_g[l]), l, 5 * dg)
        ys = [y.reshape(B * T, dg) for y in (ya, yb, yc, yd)]
        x1, h2 = _out_proj(ys, w_out[l].astype(BF16), x2, mod, row(norm_mix_post[l]),
                           row(norm_mlp_pre[l]), T)
        x2 = _mlp(h2, mlp_w1[l].astype(BF16), mlp_w2[l].astype(BF16), x1, mod,
                  row(norm_mlp_post[l]), T)
    return x2.reshape(B, T, D)
```

```python
import functools

import numpy as np
import jax
import jax.numpy as jnp
from jax import lax
from jax.experimental import pallas as pl
from jax.experimental.pallas import tpu as pltpu

F32 = jnp.float32
BF16 = jnp.bfloat16

EPS = 1e-6
N_MOD = 6
N_GROUPS = 4
POOL_WINDOWS = (2, 4, 8, 16)
POOL_HALO = 16
SGU_BLOCK = 128
CHUNK = 64
CONV_WIDTH = 31
CONV_HALO = 32
HEAD_DIM = 128
FORGET_FLOOR = 1e-20
N_LEVELS = 7

VMEM_LIMIT = 56 * 1024 * 1024


def _cparams(sem):
    return pltpu.CompilerParams(dimension_semantics=sem, vmem_limit_bytes=VMEM_LIMIT)


def _rms(x, g):
    return x * lax.rsqrt(jnp.mean(x * x, axis=-1, keepdims=True) + EPS) * g


def _layer_norm(x, g, b):
    mu = jnp.mean(x, axis=-1, keepdims=True)
    xc = x - mu
    var = jnp.mean(xc * xc, axis=-1, keepdims=True)
    return xc * lax.rsqrt(var + EPS) * g + b


def _dot(a, b):
    return jnp.dot(a, b, preferred_element_type=F32)


def _dot_nt(a, b):
    return lax.dot_general(a, b, (((1,), (1,)), ((), ())), preferred_element_type=F32)


def _dot_tn(a, b):
    return lax.dot_general(a, b, (((0,), (0,)), ((), ())), preferred_element_type=F32)


def _mod_kernel(c_ref, w_ref, b_ref, o_ref):
    c = c_ref[...]
    cond = c * jax.nn.sigmoid(c)
    o_ref[0] = jnp.dot(cond, w_ref[0], preferred_element_type=F32,
                       precision=lax.Precision.HIGHEST) + b_ref[0]


def _modulation(c, ada_w, ada_b):
    B, D = c.shape
    L, _, N = ada_w.shape
    bp = 8
    tn = 1024
    cp = jnp.zeros((bp, D), F32).at[:B].set(c)
    out = pl.pallas_call(
        _mod_kernel,
        grid=(L, N // tn),
        in_specs=[
            pl.BlockSpec((bp, D), lambda l, j: (0, 0)),
            pl.BlockSpec((1, D, tn), lambda l, j: (l, 0, j)),
            pl.BlockSpec((1, 1, tn), lambda l, j: (l, 0, j)),
        ],
        out_specs=pl.BlockSpec((1, bp, tn), lambda l, j: (l, 0, j)),
        out_shape=jax.ShapeDtypeStruct((L, bp, N), F32),
        compiler_params=_cparams(("arbitrary", "arbitrary")),
        name="ada_mod",
    )(cp, ada_w, ada_b.reshape(L, 1, N))
    return out[:, :B].reshape(L, B, N_MOD, D)


IN_ROWS = 256


def _in_proj_kernel(x_ref, mod_ref, g_ref, w_ref, z_ref):
    tm = x_ref.shape[0]
    for r in range(tm // IN_ROWS):
        rows = slice(r * IN_ROWS, (r + 1) * IN_ROWS)
        h = _rms(x_ref[rows, :], g_ref[...]) * (1.0 + mod_ref[0, 1:2, :]) + mod_ref[0, 0:1, :]
        z_ref[rows, :] = _dot(h.astype(BF16), w_ref[0]).astype(z_ref.dtype)


def _in_proj(x2, mod, g, w, layer, T, tm=512):
    M, D = x2.shape
    N = w.shape[2]
    tpb = T // tm
    return pl.pallas_call(
        _in_proj_kernel,
        grid=(M // tm,),
        in_specs=[
            pl.BlockSpec((tm, D), lambda i: (i, 0)),
            pl.BlockSpec((1, N_MOD, D), lambda i: (i // tpb, 0, 0)),
            pl.BlockSpec((1, D), lambda i: (0, 0)),
            pl.BlockSpec((1, D, N), lambda i: (layer, 0, 0), pipeline_mode=pl.Buffered(1)),
        ],
        out_specs=pl.BlockSpec((tm, N), lambda i: (i, 0)),
        out_shape=jax.ShapeDtypeStruct((M, N), BF16),
        compiler_params=_cparams(("arbitrary",)),
        name="in_proj",
    )(x2, mod, g, w)


def _pool_kernel(x_ref, halo_ref, w_ref, s_ref, o_ref, buf):
    i = pl.program_id(1)
    tt = x_ref.shape[1]
    pc = w_ref.shape[1]
    buf[POOL_HALO:POOL_HALO + tt, :] = x_ref[0].astype(F32)
    buf[0:POOL_HALO, :] = jnp.where(i == 0, 0.0, halo_ref[0].astype(F32))
    t = i * tt + lax.broadcasted_iota(jnp.int32, (tt, pc), 0)
    for gi, win in enumerate(POOL_WINDOWS):
        cols = slice(gi * pc, (gi + 1) * pc)
        x = buf[POOL_HALO:POOL_HALO + tt, cols]
        acc = x
        for k in range(1, win):
            acc = acc + buf[POOL_HALO - k:POOL_HALO - k + tt, cols]
        cnt = jnp.minimum(t + 1, win).astype(F32)
        pooled = acc / cnt - x
        y = _dot(pooled.astype(BF16), w_ref[gi]) * s_ref[:, cols]
        o_ref[0, :, cols] = y.astype(o_ref.dtype)


def _pool_mixer(z3, w, scale, tt=512):
    B, T, _ = z3.shape
    dg = scale.shape[-1]
    hb = tt // POOL_HALO
    return pl.pallas_call(
        _pool_kernel,
        grid=(B, T // tt),
        in_specs=[
            pl.BlockSpec((1, tt, dg), lambda b, i: (b, i, 0)),
            pl.BlockSpec((1, POOL_HALO, dg), lambda b, i: (b, jnp.maximum(i * hb - 1, 0), 0)),
            pl.BlockSpec(w.shape, lambda b, i: (0, 0, 0)),
            pl.BlockSpec((1, dg), lambda b, i: (0, 0)),
        ],
        out_specs=pl.BlockSpec((1, tt, dg), lambda b, i: (b, i, 0)),
        out_shape=jax.ShapeDtypeStruct((B, T, dg), BF16),
        scratch_shapes=[pltpu.VMEM((tt + POOL_HALO, dg), F32)],
        compiler_params=_cparams(("arbitrary", "arbitrary")),
        name="pool_mixer",
    )(z3, z3, w, scale)


def _sgu_kernel(u_ref, v_ref, g_ref, b_ref, w_ref, bst_ref, o_ref):
    tt = u_ref.shape[1]
    nh = w_ref.shape[0]
    hd = u_ref.shape[2] // nh
    vb = _layer_norm(v_ref[0].astype(F32), g_ref[...], b_ref[...]).astype(BF16)
    r = lax.broadcasted_iota(jnp.int32, (SGU_BLOCK, SGU_BLOCK), 0)
    c = lax.broadcasted_iota(jnp.int32, (SGU_BLOCK, SGU_BLOCK), 1)
    causal = (c // CHUNK) <= (r // CHUNK)
    for h in range(nh):
        cols = slice(h * hd, (h + 1) * hd)
        wm = jnp.where(causal, w_ref[h], 0.0).astype(BF16)
        bias = bst_ref[:, h:h + 1]
        for n in range(tt // SGU_BLOCK):
            rows = slice(n * SGU_BLOCK, (n + 1) * SGU_BLOCK)
            mixed = _dot(wm, vb[rows, cols]) + bias
            o_ref[0, rows, cols] = (u_ref[0, rows, cols].astype(F32) * mixed).astype(o_ref.dtype)


def _sgu_mixer(z3, norm_g, norm_b, ws, bs, tt=512):
    B, T, _ = z3.shape
    dg = norm_g.shape[-1]
    nh = ws.shape[0]
    return pl.pallas_call(
        _sgu_kernel,
        grid=(B, T // tt),
        in_specs=[
            pl.BlockSpec((1, tt, dg), lambda b, i: (b, i, 1)),
            pl.BlockSpec((1, tt, dg), lambda b, i: (b, i, 2)),
            pl.BlockSpec((1, dg), lambda b, i: (0, 0)),
            pl.BlockSpec((1, dg), lambda b, i: (0, 0)),
            pl.BlockSpec(ws.shape, lambda b, i: (0, 0, 0)),
            pl.BlockSpec((SGU_BLOCK, nh), lambda b, i: (0, 0)),
        ],
        out_specs=pl.BlockSpec((1, tt, dg), lambda b, i: (b, i, 0)),
        out_shape=jax.ShapeDtypeStruct((B, T, dg), BF16),
        compiler_params=_cparams(("arbitrary", "arbitrary")),
        name="sgu_mixer",
    )(z3, z3, norm_g, norm_b, ws, bs.T)


CONV_ROWS = 64
SUBLANES = 8
LANES = 128


def _conv_kernel(a_ref, g_ref, ah_ref, gh_ref, dw_ref, dwb_ref, ng_ref, nb_ref, pw_ref, pwb_ref,
                 o_ref, buf, cbuf):
    i = pl.program_id(1)
    tt = a_ref.shape[1]
    buf[CONV_HALO:CONV_HALO + tt, :] = a_ref[0].astype(F32) * jax.nn.sigmoid(g_ref[0].astype(F32))
    buf[0:CONV_HALO, :] = jnp.where(i == 0, 0.0, ah_ref[0].astype(F32) * jax.nn.sigmoid(gh_ref[0].astype(F32)))
    for cg in range(buf.shape[1] // LANES):
        cols = slice(cg * LANES, (cg + 1) * LANES)
        for rb in range(tt // CONV_ROWS):
            r0 = rb * CONV_ROWS
            hblk = buf[r0:r0 + CONV_HALO + CONV_ROWS, cols]
            acc = jnp.broadcast_to(dwb_ref[:, cols], (CONV_ROWS, LANES))
            for r in range(SUBLANES):
                hr = hblk if r == 0 else pltpu.roll(hblk, r, axis=0)
                for a in range((CONV_WIDTH - 1 - r) // SUBLANES + 1):
                    s = SUBLANES * a + r
                    k = CONV_WIDTH - 1 - s
                    lo = CONV_HALO - SUBLANES * a
                    acc = acc + hr[lo:lo + CONV_ROWS, :] * dw_ref[k:k + 1, cols]
            cbuf[r0:r0 + CONV_ROWS, cols] = acc
    h = _layer_norm(cbuf[...], ng_ref[...], nb_ref[...])
    h = h * jax.nn.sigmoid(h)
    o_ref[0] = (_dot(h.astype(BF16), pw_ref[...]) + pwb_ref[...]).astype(o_ref.dtype)


def _conv_mixer(z3, dw, dw_b, ng, nb, pw, pw_b, tt=256):
    B, T, _ = z3.shape
    dg = dw.shape[-1]
    hb = tt // CONV_HALO
    halo_map = lambda col: (lambda b, i: (b, jnp.maximum(i * hb - 1, 0), col))
    vec = pl.BlockSpec((1, dg), lambda b, i: (0, 0))
    return pl.pallas_call(
        _conv_kernel,
        grid=(B, T // tt),
        in_specs=[
            pl.BlockSpec((1, tt, dg), lambda b, i: (b, i, 3)),
            pl.BlockSpec((1, tt, dg), lambda b, i: (b, i, 4)),
            pl.BlockSpec((1, CONV_HALO, dg), halo_map(3)),
            pl.BlockSpec((1, CONV_HALO, dg), halo_map(4)),
            pl.BlockSpec(dw.shape, lambda b, i: (0, 0)),
            vec, vec, vec,
            pl.BlockSpec(pw.shape, lambda b, i: (0, 0)),
            vec,
        ],
        out_specs=pl.BlockSpec((1, tt, dg), lambda b, i: (b, i, 0)),
        out_shape=jax.ShapeDtypeStruct((B, T, dg), BF16),
        scratch_shapes=[pltpu.VMEM((tt + CONV_HALO, dg), F32), pltpu.VMEM((tt, dg), F32)],
        compiler_params=_cparams(("arbitrary", "arbitrary")),
        name="conv_mixer",
    )(z3, z3, z3, z3, dw, dw_b, ng, nb, pw, pw_b)


def _hgrn_constants():
    C = CHUNK
    r = np.arange(C)
    blocks = [(r[None, :] <= r[:, None]), (r[None, :] > r[:, None])]
    lvl = np.full((C, C), N_LEVELS, np.int32)
    lvl[r, r] = 0
    for l in range(1, N_LEVELS):
        s = C >> l
        p = r % (2 * s)
        m = r - p + s - 1
        upper = p >= s
        up = (r[None, :] > m[:, None]) & (r[None, :] <= r[:, None])
        lo = (r[None, :] > r[:, None]) & (r[None, :] <= m[:, None])
        blocks.append(np.where(upper[:, None], up, lo))
        same = (r[:, None] // (2 * s)) == (r[None, :] // (2 * s))
        lvl[same & upper[:, None] & (~upper)[None, :]] = l
    mat = np.concatenate(blocks, axis=0).astype(np.float32)
    return np.concatenate([mat, mat], axis=1), lvl


def _hgrn_kernel(q_ref, f_ref, v_ref, og_ref, lg_ref, ng_ref, mc_ref, lvl_ref, o_ref, s_scr, o_scr,
                 *, layer):
    i = pl.program_id(2)
    tt = q_ref.shape[1]

    @pl.when(i == 0)
    def _():
        s_scr[...] = jnp.zeros_like(s_scr)

    depth = lg_ref.shape[0]
    rows = [lg_ref[m:m + 1, :] for m in range(depth)]
    mx = functools.reduce(jnp.maximum, rows)
    es = [jnp.exp(row - mx) for row in rows]
    den = functools.reduce(lambda a, b: a + b, es)
    sm = [e / den for e in es]
    lb = functools.reduce(lambda a, b: a + b, sm[:layer + 1]) - sm[0]

    fz = f_ref[0].astype(F32)
    f = lb + (1.0 - lb) * jax.nn.sigmoid(fz)
    logf = jnp.log(jnp.maximum(f, FORGET_FLOOR))
    kk = (1.0 - lb) * jax.nn.sigmoid(-fz)
    lvl = lvl_ref[...]
    scale = HEAD_DIM ** -0.5
    nhead = q_ref.shape[2] // HEAD_DIM
    nchunk = tt // CHUNK

    hi = logf.astype(BF16)
    lo = (logf - hi.astype(F32)).astype(BF16)
    e_all = []
    for c in range(nchunk):
        rows = slice(c * CHUNK, (c + 1) * CHUNK)
        e_all.append(_dot(mc_ref[...], jnp.concatenate([hi[rows], lo[rows]], axis=0)))

    units = [(c, h) for c in range(nchunk) for h in range(nhead)]
    rows_of = lambda c: slice(c * CHUNK, (c + 1) * CHUNK)
    cols_of = lambda h: slice(h * HEAD_DIM, (h + 1) * HEAD_DIM)
    level_products, qes, upds, decays = {}, {}, {}, {}
    for c, h in units:
        e_c = e_all[c][:, cols_of(h)]
        qc = q_ref[0, rows_of(c), cols_of(h)].astype(F32) * scale
        kc = kk[rows_of(c), cols_of(h)]
        prods = [_dot_nt(qc.astype(BF16), kc.astype(BF16))]
        for l in range(1, N_LEVELS):
            ex = jnp.exp(e_c[(l + 1) * CHUNK:(l + 2) * CHUNK])
            prods.append(_dot_nt((qc * ex).astype(BF16), (kc * ex).astype(BF16)))
        level_products[c, h] = prods
        qes[c, h] = (qc * jnp.exp(e_c[0:CHUNK])).astype(BF16)
        kd = (kc * jnp.exp(e_c[CHUNK:2 * CHUNK])).astype(BF16)
        upds[c, h] = _dot_tn(v_ref[0, rows_of(c), cols_of(h)], kd)
        decays[c, h] = jnp.exp(e_c[CHUNK - 1:CHUNK, :])
    for c, h in units:
        prods = level_products[c, h]
        att = jnp.where(lvl == 0, prods[0], 0.0)
        for l in range(1, N_LEVELS):
            att = jnp.where(lvl == l, prods[l], att)
        o_scr[rows_of(c), cols_of(h)] = _dot(att.astype(BF16), v_ref[0, rows_of(c), cols_of(h)])
    for h in range(nhead):
        st = s_scr[h]
        for c in range(nchunk):
            o_scr[rows_of(c), cols_of(h)] += _dot_nt(qes[c, h], st.astype(BF16))
            st = st * decays[c, h] + upds[c, h]
        s_scr[h] = st

    og = og_ref[0].astype(F32)
    for h in range(nhead):
        cols = slice(h * HEAD_DIM, (h + 1) * HEAD_DIM)
        o = _rms(o_scr[:, cols], ng_ref[:, cols])
        o_ref[0, :, cols] = (o * (og[:, cols] * jax.nn.sigmoid(og[:, cols]))).astype(o_ref.dtype)


HGRN_HEADS_PER_STEP = 4


def _hgrn_mixer(z3, lb_logits, norm_g, layer, col0, tt=512):
    B, T, _ = z3.shape
    dg = norm_g.shape[-1]
    hw = HGRN_HEADS_PER_STEP * HEAD_DIM
    ng = dg // hw
    c0 = col0 // hw
    mat, lvl = _hgrn_constants()
    part = lambda p: pl.BlockSpec((1, tt, hw), lambda b, h, i: (b, i, c0 + p * ng + h))
    return pl.pallas_call(
        functools.partial(_hgrn_kernel, layer=layer),
        grid=(B, ng, T // tt),
        in_specs=[
            part(0), part(1), part(2), part(3),
            pl.BlockSpec((lb_logits.shape[0], hw), lambda b, h, i: (0, h)),
            pl.BlockSpec((1, hw), lambda b, h, i: (0, h)),
            pl.BlockSpec(mat.shape, lambda b, h, i: (0, 0)),
            pl.BlockSpec(lvl.shape, lambda b, h, i: (0, 0)),
        ],
        out_specs=pl.BlockSpec((1, tt, hw), lambda b, h, i: (b, i, h)),
        out_shape=jax.ShapeDtypeStruct((B, T, dg), BF16),
        scratch_shapes=[pltpu.VMEM((HGRN_HEADS_PER_STEP, HEAD_DIM, HEAD_DIM), F32),
                        pltpu.VMEM((tt, hw), F32)],
        compiler_params=_cparams(("arbitrary", "arbitrary", "arbitrary")),
        name="hgrn_mixer",
    )(z3, z3, z3, z3, lb_logits, norm_g, jnp.asarray(mat, BF16), jnp.asarray(lvl))


OUT_ROWS = 256


def _out_proj_kernel(ya_ref, yb_ref, yc_ref, yd_ref, w_ref, x_ref, mod_ref, gpost_ref, gpre_ref,
                     x1_ref, h2_ref, cat):
    tm = x_ref.shape[0]
    dg = ya_ref.shape[1]
    for p, ref in enumerate((ya_ref, yb_ref, yc_ref, yd_ref)):
        cat[:, p * dg:(p + 1) * dg] = ref[...]
    for r in range(tm // OUT_ROWS):
        rows = slice(r * OUT_ROWS, (r + 1) * OUT_ROWS)
        y = _dot(cat[rows, :], w_ref[0])
        x1 = x_ref[rows, :] + mod_ref[0, 2:3, :] * _rms(y, gpost_ref[...])
        x1_ref[rows, :] = x1
        h2 = _rms(x1, gpre_ref[...]) * (1.0 + mod_ref[0, 4:5, :]) + mod_ref[0, 3:4, :]
        h2_ref[rows, :] = h2.astype(h2_ref.dtype)


def _out_proj(ys, w, layer, x2, mod, g_post, g_pre, T, tm=512):
    M, D = x2.shape
    dg = ys[0].shape[1]
    tpb = T // tm
    yspec = pl.BlockSpec((tm, dg), lambda i: (i, 0))
    vec = pl.BlockSpec((1, D), lambda i: (0, 0))
    return pl.pallas_call(
        _out_proj_kernel,
        grid=(M // tm,),
        in_specs=[
            yspec, yspec, yspec, yspec,
            pl.BlockSpec((1,) + w.shape[1:], lambda i: (layer, 0, 0), pipeline_mode=pl.Buffered(1)),
            pl.BlockSpec((tm, D), lambda i: (i, 0)),
            pl.BlockSpec((1, N_MOD, D), lambda i: (i // tpb, 0, 0)),
            vec, vec,
        ],
        out_specs=[pl.BlockSpec((tm, D), lambda i: (i, 0)), pl.BlockSpec((tm, D), lambda i: (i, 0))],
        out_shape=[jax.ShapeDtypeStruct((M, D), F32), jax.ShapeDtypeStruct((M, D), BF16)],
        scratch_shapes=[pltpu.VMEM((tm, len(ys) * dg), BF16)],
        compiler_params=_cparams(("arbitrary",)),
        name="out_proj",
    )(*ys, w, x2, mod, g_post, g_pre)


def _mlp_kernel(h_ref, w1_ref, w2_ref, x1_ref, mod_ref, g_ref, o_ref):
    j = pl.program_id(1)

    @pl.when(j == 0)
    def _():
        o_ref[...] = jnp.zeros_like(o_ref)

    a = _dot(h_ref[...], w1_ref[0])
    a = jnp.square(jnp.maximum(a, 0.0)).astype(BF16)
    o_ref[...] += _dot(a, w2_ref[0])

    @pl.when(j == pl.num_programs(1) - 1)
    def _():
        o_ref[...] = x1_ref[...] + mod_ref[0, 5:6, :] * _rms(o_ref[...], g_ref[...])


def _mlp(h2, w1, w2, layer, x1, mod, g, T, tm=512, tf=1024):
    M, D = x1.shape
    FF = w1.shape[2]
    tpb = T // tm
    return pl.pallas_call(
        _mlp_kernel,
        grid=(M // tm, FF // tf),
        in_specs=[
            pl.BlockSpec((tm, D), lambda i, j: (i, 0)),
            pl.BlockSpec((1, D, tf), lambda i, j: (layer, 0, j)),
            pl.BlockSpec((1, tf, D), lambda i, j: (layer, j, 0)),
            pl.BlockSpec((tm, D), lambda i, j: (i, 0)),
            pl.BlockSpec((1, N_MOD, D), lambda i, j: (i // tpb, 0, 0)),
            pl.BlockSpec((1, D), lambda i, j: (0, 0)),
        ],
        out_specs=pl.BlockSpec((tm, D), lambda i, j: (i, 0)),
        out_shape=jax.ShapeDtypeStruct((M, D), F32),
        compiler_params=_cparams(("arbitrary", "arbitrary")),
        name="mlp",
    )(h2, w1, w2, x1, mod, g)


def kernel(x, c, ada_w, ada_b, norm_mix_pre, norm_mix_post, norm_mlp_pre, norm_mlp_post, w_in, pool_w, pool_scale, sgu_norm_g, sgu_norm_b, sgu_w, sgu_b, conv_dw, conv_dw_b, conv_norm_g, conv_norm_b, conv_pw, conv_pw_b, hgrn_lb_logits, hgrn_norm_g, w_out, mlp_w1, mlp_w2):
    B, T, D = x.shape
    L = ada_w.shape[0]
    dg = D // N_GROUPS
    row = lambda a: a.reshape(1, -1)

    mod_all = _modulation(c, ada_w, ada_b)
    w_in_b, w_out_b, w1_b, w2_b = (w.astype(BF16) for w in (w_in, w_out, mlp_w1, mlp_w2))
    x2 = x.reshape(B * T, D)
    for l in range(L):
        mod = mod_all[l]
        z = _in_proj(x2, mod, row(norm_mix_pre[l]), w_in_b, l, T)
        z3 = z.reshape(B, T, -1)
        ya = _pool_mixer(z3, pool_w[l].astype(BF16), row(pool_scale[l]))
        yb = _sgu_mixer(z3, row(sgu_norm_g[l]), row(sgu_norm_b[l]), sgu_w[l], sgu_b[l])
        yc = _conv_mixer(z3, conv_dw[l], row(conv_dw_b[l]), row(conv_norm_g[l]), row(conv_norm_b[l]),
                         conv_pw[l].astype(BF16), row(conv_pw_b[l]))
        yd = _hgrn_mixer(z3, hgrn_lb_logits, row(hgrn_norm_g[l]), l, 5 * dg)
        ys = [y.reshape(B * T, dg) for y in (ya, yb, yc, yd)]
        x1, h2 = _out_proj(ys, w_out_b, l, x2, mod, row(norm_mix_post[l]), row(norm_mlp_pre[l]), T)
        x2 = _mlp(h2, w1_b, w2_b, l, x1, mod, row(norm_mlp_post[l]), T)
    return x2.reshape(B, T, D)
```

```python
import functools

import numpy as np
import jax
import jax.numpy as jnp
from jax import lax
from jax.experimental import pallas as pl
from jax.experimental.pallas import tpu as pltpu

F32 = jnp.float32
BF16 = jnp.bfloat16

EPS = 1e-6
N_MOD = 6
N_GROUPS = 4
POOL_WINDOWS = (2, 4, 8, 16)
POOL_HALO = 16
SGU_BLOCK = 128
CHUNK = 64
CONV_WIDTH = 31
CONV_HALO = 32
HEAD_DIM = 128
FORGET_FLOOR = 1e-20
N_LEVELS = 7

VMEM_LIMIT = 56 * 1024 * 1024


def _cparams(sem):
    return pltpu.CompilerParams(dimension_semantics=sem, vmem_limit_bytes=VMEM_LIMIT)


def _rms(x, g):
    return x * lax.rsqrt(jnp.mean(x * x, axis=-1, keepdims=True) + EPS) * g


def _layer_norm(x, g, b):
    mu = jnp.mean(x, axis=-1, keepdims=True)
    xc = x - mu
    var = jnp.mean(xc * xc, axis=-1, keepdims=True)
    return xc * lax.rsqrt(var + EPS) * g + b


def _dot(a, b):
    return jnp.dot(a, b, preferred_element_type=F32)


def _dot_nt(a, b):
    return lax.dot_general(a, b, (((1,), (1,)), ((), ())), preferred_element_type=F32)


def _dot_tn(a, b):
    return lax.dot_general(a, b, (((0,), (0,)), ((), ())), preferred_element_type=F32)


def _mod_kernel(c_ref, w_ref, b_ref, o_ref):
    c = c_ref[...]
    cond = c * jax.nn.sigmoid(c)
    o_ref[0] = jnp.dot(cond, w_ref[0], preferred_element_type=F32,
                       precision=lax.Precision.HIGHEST) + b_ref[0]


def _modulation(c, ada_w, ada_b):
    B, D = c.shape
    L, _, N = ada_w.shape
    bp = 8
    tn = 1024
    cp = jnp.zeros((bp, D), F32).at[:B].set(c)
    out = pl.pallas_call(
        _mod_kernel,
        grid=(L, N // tn),
        in_specs=[
            pl.BlockSpec((bp, D), lambda l, j: (0, 0)),
            pl.BlockSpec((1, D, tn), lambda l, j: (l, 0, j)),
            pl.BlockSpec((1, 1, tn), lambda l, j: (l, 0, j)),
        ],
        out_specs=pl.BlockSpec((1, bp, tn), lambda l, j: (l, 0, j)),
        out_shape=jax.ShapeDtypeStruct((L, bp, N), F32),
        compiler_params=_cparams(("arbitrary", "arbitrary")),
        name="ada_mod",
    )(cp, ada_w, ada_b.reshape(L, 1, N))
    return out[:, :B].reshape(L, B, N_MOD, D)


IN_ROWS = 256


def _in_proj_kernel(x_ref, mod_ref, g_ref, w_ref, z_ref):
    tm = x_ref.shape[0]
    gain = g_ref[...] * (1.0 + mod_ref[0, 1:2, :])
    for r in range(tm // IN_ROWS):
        rows = slice(r * IN_ROWS, (r + 1) * IN_ROWS)
        x = x_ref[rows, :]
        h = x * lax.rsqrt(jnp.mean(x * x, axis=-1, keepdims=True) + EPS) * gain + mod_ref[0, 0:1, :]
        z_ref[rows, :] = _dot(h.astype(BF16), w_ref[0]).astype(z_ref.dtype)


def _in_proj(x2, mod, g, w, layer, T, tm=512):
    M, D = x2.shape
    N = w.shape[2]
    tpb = T // tm
    return pl.pallas_call(
        _in_proj_kernel,
        grid=(M // tm,),
        in_specs=[
            pl.BlockSpec((tm, D), lambda i: (i, 0)),
            pl.BlockSpec((1, N_MOD, D), lambda i: (i // tpb, 0, 0)),
            pl.BlockSpec((1, D), lambda i: (0, 0)),
            pl.BlockSpec((1, D, N), lambda i: (layer, 0, 0), pipeline_mode=pl.Buffered(1)),
        ],
        out_specs=pl.BlockSpec((tm, N), lambda i: (i, 0)),
        out_shape=jax.ShapeDtypeStruct((M, N), BF16),
        compiler_params=_cparams(("arbitrary",)),
        name="in_proj",
    )(x2, mod, g, w)


def _pool_kernel(x_ref, halo_ref, w_ref, s_ref, o_ref, buf):
    i = pl.program_id(1)
    tt = x_ref.shape[1]
    pc = w_ref.shape[1]
    buf[POOL_HALO:POOL_HALO + tt, :] = x_ref[0].astype(F32)
    buf[0:POOL_HALO, :] = jnp.where(i == 0, 0.0, halo_ref[0].astype(F32))
    t = i * tt + lax.broadcasted_iota(jnp.int32, (tt, pc), 0)
    for gi, win in enumerate(POOL_WINDOWS):
        cols = slice(gi * pc, (gi + 1) * pc)
        x = buf[POOL_HALO:POOL_HALO + tt, cols]
        acc = x
        for k in range(1, win):
            acc = acc + buf[POOL_HALO - k:POOL_HALO - k + tt, cols]
        cnt = jnp.minimum(t + 1, win).astype(F32)
        pooled = acc / cnt - x
        y = _dot(pooled.astype(BF16), w_ref[gi]) * s_ref[:, cols]
        o_ref[0, :, cols] = y.astype(o_ref.dtype)


def _pool_mixer(z3, w, scale, tt=512):
    B, T, _ = z3.shape
    dg = scale.shape[-1]
    hb = tt // POOL_HALO
    return pl.pallas_call(
        _pool_kernel,
        grid=(B, T // tt),
        in_specs=[
            pl.BlockSpec((1, tt, dg), lambda b, i: (b, i, 0)),
            pl.BlockSpec((1, POOL_HALO, dg), lambda b, i: (b, jnp.maximum(i * hb - 1, 0), 0)),
            pl.BlockSpec(w.shape, lambda b, i: (0, 0, 0)),
            pl.BlockSpec((1, dg), lambda b, i: (0, 0)),
        ],
        out_specs=pl.BlockSpec((1, tt, dg), lambda b, i: (b, i, 0)),
        out_shape=jax.ShapeDtypeStruct((B, T, dg), BF16),
        scratch_shapes=[pltpu.VMEM((tt + POOL_HALO, dg), F32)],
        compiler_params=_cparams(("arbitrary", "arbitrary")),
        name="pool_mixer",
    )(z3, z3, w, scale)


def _sgu_kernel(u_ref, v_ref, g_ref, b_ref, w_ref, bst_ref, o_ref):
    tt = u_ref.shape[1]
    nh = w_ref.shape[0]
    hd = u_ref.shape[2] // nh
    vb = _layer_norm(v_ref[0].astype(F32), g_ref[...], b_ref[...]).astype(BF16)
    r = lax.broadcasted_iota(jnp.int32, (SGU_BLOCK, SGU_BLOCK), 0)
    c = lax.broadcasted_iota(jnp.int32, (SGU_BLOCK, SGU_BLOCK), 1)
    causal = (c // CHUNK) <= (r // CHUNK)
    for h in range(nh):
        cols = slice(h * hd, (h + 1) * hd)
        wm = jnp.where(causal, w_ref[h], 0.0).astype(BF16)
        bias = bst_ref[:, h:h + 1]
        for n in range(tt // SGU_BLOCK):
            rows = slice(n * SGU_BLOCK, (n + 1) * SGU_BLOCK)
            mixed = _dot(wm, vb[rows, cols]) + bias
            o_ref[0, rows, cols] = (u_ref[0, rows, cols].astype(F32) * mixed).astype(o_ref.dtype)


def _sgu_mixer(z3, norm_g, norm_b, ws, bs, tt=512):
    B, T, _ = z3.shape
    dg = norm_g.shape[-1]
    nh = ws.shape[0]
    return pl.pallas_call(
        _sgu_kernel,
        grid=(B, T // tt),
        in_specs=[
            pl.BlockSpec((1, tt, dg), lambda b, i: (b, i, 1)),
            pl.BlockSpec((1, tt, dg), lambda b, i: (b, i, 2)),
            pl.BlockSpec((1, dg), lambda b, i: (0, 0)),
            pl.BlockSpec((1, dg), lambda b, i: (0, 0)),
            pl.BlockSpec(ws.shape, lambda b, i: (0, 0, 0)),
            pl.BlockSpec((SGU_BLOCK, nh), lambda b, i: (0, 0)),
        ],
        out_specs=pl.BlockSpec((1, tt, dg), lambda b, i: (b, i, 0)),
        out_shape=jax.ShapeDtypeStruct((B, T, dg), BF16),
        compiler_params=_cparams(("arbitrary", "arbitrary")),
        name="sgu_mixer",
    )(z3, z3, norm_g, norm_b, ws, bs.T)


CONV_ROWS = 64
SUBLANES = 8
LANES = 128


def _conv_kernel(a_ref, g_ref, ah_ref, gh_ref, dw_ref, dwb_ref, ng_ref, nb_ref, pw_ref, pwb_ref,
                 o_ref, buf, cbuf):
    i = pl.program_id(1)
    tt = a_ref.shape[1]
    buf[CONV_HALO:CONV_HALO + tt, :] = a_ref[0].astype(F32) * jax.nn.sigmoid(g_ref[0].astype(F32))
    buf[0:CONV_HALO, :] = jnp.where(i == 0, 0.0, ah_ref[0].astype(F32) * jax.nn.sigmoid(gh_ref[0].astype(F32)))
    for cg in range(buf.shape[1] // LANES):
        cols = slice(cg * LANES, (cg + 1) * LANES)
        for rb in range(tt // CONV_ROWS):
            r0 = rb * CONV_ROWS
            hblk = buf[r0:r0 + CONV_HALO + CONV_ROWS, cols]
            acc = jnp.broadcast_to(dwb_ref[:, cols], (CONV_ROWS, LANES))
            for r in range(SUBLANES):
                hr = hblk if r == 0 else pltpu.roll(hblk, r, axis=0)
                for a in range((CONV_WIDTH - 1 - r) // SUBLANES + 1):
                    s = SUBLANES * a + r
                    k = CONV_WIDTH - 1 - s
                    lo = CONV_HALO - SUBLANES * a
                    acc = acc + hr[lo:lo + CONV_ROWS, :] * dw_ref[k:k + 1, cols]
            cbuf[r0:r0 + CONV_ROWS, cols] = acc
    h = _layer_norm(cbuf[...], ng_ref[...], nb_ref[...])
    h = h * jax.nn.sigmoid(h)
    o_ref[0] = (_dot(h.astype(BF16), pw_ref[...]) + pwb_ref[...]).astype(o_ref.dtype)


def _conv_mixer(z3, dw, dw_b, ng, nb, pw, pw_b, tt=256):
    B, T, _ = z3.shape
    dg = dw.shape[-1]
    hb = tt // CONV_HALO
    halo_map = lambda col: (lambda b, i: (b, jnp.maximum(i * hb - 1, 0), col))
    vec = pl.BlockSpec((1, dg), lambda b, i: (0, 0))
    return pl.pallas_call(
        _conv_kernel,
        grid=(B, T // tt),
        in_specs=[
            pl.BlockSpec((1, tt, dg), lambda b, i: (b, i, 3)),
            pl.BlockSpec((1, tt, dg), lambda b, i: (b, i, 4)),
            pl.BlockSpec((1, CONV_HALO, dg), halo_map(3)),
            pl.BlockSpec((1, CONV_HALO, dg), halo_map(4)),
            pl.BlockSpec(dw.shape, lambda b, i: (0, 0)),
            vec, vec, vec,
            pl.BlockSpec(pw.shape, lambda b, i: (0, 0)),
            vec,
        ],
        out_specs=pl.BlockSpec((1, tt, dg), lambda b, i: (b, i, 0)),
        out_shape=jax.ShapeDtypeStruct((B, T, dg), BF16),
        scratch_shapes=[pltpu.VMEM((tt + CONV_HALO, dg), F32), pltpu.VMEM((tt, dg), F32)],
        compiler_params=_cparams(("arbitrary", "arbitrary")),
        name="conv_mixer",
    )(z3, z3, z3, z3, dw, dw_b, ng, nb, pw, pw_b)


def _hgrn_constants():
    C = CHUNK
    r = np.arange(C)
    blocks = [(r[None, :] <= r[:, None])]
    lvl = np.full((C, C), N_LEVELS, np.int32)
    lvl[r, r] = 0
    for l in range(1, N_LEVELS):
        s = C >> l
        p = r % (2 * s)
        m = r - p + s - 1
        upper = p >= s
        up = (r[None, :] > m[:, None]) & (r[None, :] <= r[:, None])
        lo = (r[None, :] > r[:, None]) & (r[None, :] <= m[:, None])
        if s < SUBLANES:
            blocks.append(np.where(upper[:, None], up, lo))
        same = (r[:, None] // (2 * s)) == (r[None, :] // (2 * s))
        lvl[same & upper[:, None] & (~upper)[None, :]] = l
    mat = np.concatenate(blocks, axis=0).astype(np.float32)
    return np.concatenate([mat, mat], axis=1), lvl


def _level_exponent(b, s):
    parts = []
    for base in range(0, CHUNK, 2 * s):
        ref = jnp.broadcast_to(b[base + s - 1:base + s], (s, b.shape[1]))
        parts.append(ref - b[base:base + s])
        parts.append(b[base + s:base + 2 * s] - ref)
    return jnp.concatenate(parts, axis=0)


def _hgrn_kernel(q_ref, f_ref, v_ref, og_ref, lg_ref, ng_ref, mc_ref, lvl_ref, o_ref, s_scr, o_scr,
                 *, layer):
    i = pl.program_id(2)
    tt = q_ref.shape[1]

    @pl.when(i == 0)
    def _():
        s_scr[...] = jnp.zeros_like(s_scr)

    depth = lg_ref.shape[0]
    rows = [lg_ref[m:m + 1, :] for m in range(depth)]
    mx = functools.reduce(jnp.maximum, rows)
    es = [jnp.exp(row - mx) for row in rows]
    den = functools.reduce(lambda a, b: a + b, es)
    sm = [e / den for e in es]
    lb = functools.reduce(lambda a, b: a + b, sm[:layer + 1]) - sm[0]

    fz = f_ref[0].astype(F32)
    f = lb + (1.0 - lb) * jax.nn.sigmoid(fz)
    logf = jnp.log2(jnp.maximum(f, FORGET_FLOOR))
    kk = (1.0 - lb) * jax.nn.sigmoid(-fz)
    lvl = lvl_ref[...]
    scale = HEAD_DIM ** -0.5
    nhead = q_ref.shape[2] // HEAD_DIM
    nchunk = tt // CHUNK
    n_fine = mc_ref.shape[0] // CHUNK - 1

    hi = logf.astype(BF16)
    lo = (logf - hi.astype(F32)).astype(BF16)
    e_all = []
    for c in range(nchunk):
        rows = slice(c * CHUNK, (c + 1) * CHUNK)
        e_all.append(_dot(mc_ref[...], jnp.concatenate([hi[rows], lo[rows]], axis=0)))

    units = [(c, h) for c in range(nchunk) for h in range(nhead)]
    rows_of = lambda c: slice(c * CHUNK, (c + 1) * CHUNK)
    cols_of = lambda h: slice(h * HEAD_DIM, (h + 1) * HEAD_DIM)
    level_products, qes, upds, decays = {}, {}, {}, {}
    for c, h in units:
        e_c = e_all[c][:, cols_of(h)]
        bcum = e_c[0:CHUNK]
        qc = q_ref[0, rows_of(c), cols_of(h)].astype(F32) * scale
        kc = kk[rows_of(c), cols_of(h)]
        qb, kb = qc.astype(BF16), kc.astype(BF16)
        prods = [jnp.sum(qc * kc, axis=-1, keepdims=True)]
        for l in range(1, N_LEVELS):
            s = CHUNK >> l
            if s >= SUBLANES:
                e_l = _level_exponent(bcum, s)
            else:
                fine = l - (N_LEVELS - n_fine)
                e_l = e_c[(fine + 1) * CHUNK:(fine + 2) * CHUNK]
            ex = jnp.exp2(e_l).astype(BF16)
            prods.append(_dot_nt(qb * ex, kb * ex))
        level_products[c, h] = prods
        qes[c, h] = (qc * jnp.exp2(bcum)).astype(BF16)
        b_last = bcum[CHUNK - 1:CHUNK, :]
        kd = (kc * jnp.exp2(b_last - bcum)).astype(BF16)
        upds[c, h] = _dot_tn(v_ref[0, rows_of(c), cols_of(h)], kd)
        decays[c, h] = jnp.exp2(b_last)
    is_level = [lvl == l for l in range(N_LEVELS)]
    for c, h in units:
        prods = level_products[c, h]
        att = jnp.where(is_level[0], prods[0], 0.0)
        for l in range(1, N_LEVELS):
            att = jnp.where(is_level[l], prods[l], att)
        o_scr[rows_of(c), cols_of(h)] = _dot(att.astype(BF16), v_ref[0, rows_of(c), cols_of(h)])
    for h in range(nhead):
        st = s_scr[h]
        for c in range(nchunk):
            o_scr[rows_of(c), cols_of(h)] += _dot_nt(qes[c, h], st.astype(BF16))
            st = st * decays[c, h] + upds[c, h]
        s_scr[h] = st

    og = og_ref[0].astype(F32)
    for h in range(nhead):
        cols = slice(h * HEAD_DIM, (h + 1) * HEAD_DIM)
        o = _rms(o_scr[:, cols], ng_ref[:, cols])
        o_ref[0, :, cols] = (o * (og[:, cols] * jax.nn.sigmoid(og[:, cols]))).astype(o_ref.dtype)


HGRN_HEADS_PER_STEP = 4


def _hgrn_mixer(z3, lb_logits, norm_g, layer, col0, tt=512):
    B, T, _ = z3.shape
    dg = norm_g.shape[-1]
    hw = HGRN_HEADS_PER_STEP * HEAD_DIM
    ng = dg // hw
    c0 = col0 // hw
    mat, lvl = _hgrn_constants()
    part = lambda p: pl.BlockSpec((1, tt, hw), lambda b, h, i: (b, i, c0 + p * ng + h))
    return pl.pallas_call(
        functools.partial(_hgrn_kernel, layer=layer),
        grid=(B, ng, T // tt),
        in_specs=[
            part(0), part(1), part(2), part(3),
            pl.BlockSpec((lb_logits.shape[0], hw), lambda b, h, i: (0, h)),
            pl.BlockSpec((1, hw), lambda b, h, i: (0, h)),
            pl.BlockSpec(mat.shape, lambda b, h, i: (0, 0)),
            pl.BlockSpec(lvl.shape, lambda b, h, i: (0, 0)),
        ],
        out_specs=pl.BlockSpec((1, tt, hw), lambda b, h, i: (b, i, h)),
        out_shape=jax.ShapeDtypeStruct((B, T, dg), BF16),
        scratch_shapes=[pltpu.VMEM((HGRN_HEADS_PER_STEP, HEAD_DIM, HEAD_DIM), F32),
                        pltpu.VMEM((tt, hw), F32)],
        compiler_params=_cparams(("arbitrary", "arbitrary", "arbitrary")),
        name="hgrn_mixer",
    )(z3, z3, z3, z3, lb_logits, norm_g, jnp.asarray(mat, BF16), jnp.asarray(lvl))


OUT_ROWS = 256


def _out_proj_kernel(ya_ref, yb_ref, yc_ref, yd_ref, w_ref, x_ref, mod_ref, gpost_ref, gpre_ref,
                     x1_ref, h2_ref, cat):
    tm = x_ref.shape[0]
    dg = ya_ref.shape[1]
    for p, ref in enumerate((ya_ref, yb_ref, yc_ref, yd_ref)):
        cat[:, p * dg:(p + 1) * dg] = ref[...]
    gain_post = mod_ref[0, 2:3, :] * gpost_ref[...]
    gain_pre = gpre_ref[...] * (1.0 + mod_ref[0, 4:5, :])
    for r in range(tm // OUT_ROWS):
        rows = slice(r * OUT_ROWS, (r + 1) * OUT_ROWS)
        y = _dot(cat[rows, :], w_ref[0])
        x1 = x_ref[rows, :] + y * lax.rsqrt(jnp.mean(y * y, axis=-1, keepdims=True) + EPS) * gain_post
        x1_ref[rows, :] = x1
        h2 = x1 * lax.rsqrt(jnp.mean(x1 * x1, axis=-1, keepdims=True) + EPS) * gain_pre + mod_ref[0, 3:4, :]
        h2_ref[rows, :] = h2.astype(h2_ref.dtype)


def _out_proj(ys, w, layer, x2, mod, g_post, g_pre, T, tm=512):
    M, D = x2.shape
    dg = ys[0].shape[1]
    tpb = T // tm
    yspec = pl.BlockSpec((tm, dg), lambda i: (i, 0))
    vec = pl.BlockSpec((1, D), lambda i: (0, 0))
    return pl.pallas_call(
        _out_proj_kernel,
        grid=(M // tm,),
        in_specs=[
            yspec, yspec, yspec, yspec,
            pl.BlockSpec((1,) + w.shape[1:], lambda i: (layer, 0, 0), pipeline_mode=pl.Buffered(1)),
            pl.BlockSpec((tm, D), lambda i: (i, 0)),
            pl.BlockSpec((1, N_MOD, D), lambda i: (i // tpb, 0, 0)),
            vec, vec,
        ],
        out_specs=[pl.BlockSpec((tm, D), lambda i: (i, 0)), pl.BlockSpec((tm, D), lambda i: (i, 0))],
        out_shape=[jax.ShapeDtypeStruct((M, D), F32), jax.ShapeDtypeStruct((M, D), BF16)],
        scratch_shapes=[pltpu.VMEM((tm, len(ys) * dg), BF16)],
        compiler_params=_cparams(("arbitrary",)),
        name="out_proj",
    )(*ys, w, x2, mod, g_post, g_pre)


MLP_ROWS = 256


def _mlp_kernel(h_ref, w1_ref, w2_ref, x1_ref, mod_ref, g_ref, o_ref):
    j = pl.program_id(1)

    @pl.when(j == 0)
    def _():
        o_ref[...] = jnp.zeros_like(o_ref)

    tm = h_ref.shape[0]
    blocks = [slice(r * MLP_ROWS, (r + 1) * MLP_ROWS) for r in range(tm // MLP_ROWS)]
    acts = [_dot(h_ref[rows, :], w1_ref[0]) for rows in blocks]
    for rows, a in zip(blocks, acts):
        a = jnp.square(jnp.maximum(a, 0.0)).astype(BF16)
        o_ref[rows, :] += _dot(a, w2_ref[0])

    @pl.when(j == pl.num_programs(1) - 1)
    def _():
        gain = mod_ref[0, 5:6, :] * g_ref[...]
        y = o_ref[...]
        o_ref[...] = x1_ref[...] + y * lax.rsqrt(jnp.mean(y * y, axis=-1, keepdims=True) + EPS) * gain


def _mlp(h2, w1, w2, layer, x1, mod, g, T, tm=512, tf=1024):
    M, D = x1.shape
    FF = w1.shape[2]
    tpb = T // tm
    return pl.pallas_call(
        _mlp_kernel,
        grid=(M // tm, FF // tf),
        in_specs=[
            pl.BlockSpec((tm, D), lambda i, j: (i, 0)),
            pl.BlockSpec((1, D, tf), lambda i, j: (layer, 0, j)),
            pl.BlockSpec((1, tf, D), lambda i, j: (layer, j, 0)),
            pl.BlockSpec((tm, D), lambda i, j: (i, 0)),
            pl.BlockSpec((1, N_MOD, D), lambda i, j: (i // tpb, 0, 0)),
            pl.BlockSpec((1, D), lambda i, j: (0, 0)),
        ],
        out_specs=pl.BlockSpec((tm, D), lambda i, j: (i, 0)),
        out_shape=jax.ShapeDtypeStruct((M, D), F32),
        compiler_params=_cparams(("arbitrary", "arbitrary")),
        name="mlp",
    )(h2, w1, w2, x1, mod, g)


def kernel(x, c, ada_w, ada_b, norm_mix_pre, norm_mix_post, norm_mlp_pre, norm_mlp_post, w_in, pool_w, pool_scale, sgu_norm_g, sgu_norm_b, sgu_w, sgu_b, conv_dw, conv_dw_b, conv_norm_g, conv_norm_b, conv_pw, conv_pw_b, hgrn_lb_logits, hgrn_norm_g, w_out, mlp_w1, mlp_w2):
    B, T, D = x.shape
    L = ada_w.shape[0]
    dg = D // N_GROUPS
    row = lambda a: a.reshape(1, -1)

    mod_all = _modulation(c, ada_w, ada_b)
    w_in_b, w_out_b, w1_b, w2_b = (w.astype(BF16) for w in (w_in, w_out, mlp_w1, mlp_w2))
    x2 = x.reshape(B * T, D)
    for l in range(L):
        mod = mod_all[l]
        z = _in_proj(x2, mod, row(norm_mix_pre[l]), w_in_b, l, T)
        z3 = z.reshape(B, T, -1)
        ya = _pool_mixer(z3, pool_w[l].astype(BF16), row(pool_scale[l]))
        yb = _sgu_mixer(z3, row(sgu_norm_g[l]), row(sgu_norm_b[l]), sgu_w[l], sgu_b[l])
        yc = _conv_mixer(z3, conv_dw[l], row(conv_dw_b[l]), row(conv_norm_g[l]), row(conv_norm_b[l]),
                         conv_pw[l].astype(BF16), row(conv_pw_b[l]))
        yd = _hgrn_mixer(z3, hgrn_lb_logits, row(hgrn_norm_g[l]), l, 5 * dg)
        ys = [y.reshape(B * T, dg) for y in (ya, yb, yc, yd)]
        x1, h2 = _out_proj(ys, w_out_b, l, x2, mod, row(norm_mix_post[l]), row(norm_mlp_pre[l]), T)
        x2 = _mlp(h2, w1_b, w2_b, l, x1, mod, row(norm_mlp_post[l]), T)
    return x2.reshape(B, T, D)
```

```python
import functools

import numpy as np
import jax
import jax.numpy as jnp
from jax import lax
from jax.experimental import pallas as pl
from jax.experimental.pallas import tpu as pltpu

F32 = jnp.float32
BF16 = jnp.bfloat16

EPS = 1e-6
N_MOD = 6
N_GROUPS = 4
POOL_WINDOWS = (2, 4, 8, 16)
POOL_HALO = 16
SGU_BLOCK = 128
CHUNK = 64
CONV_WIDTH = 31
CONV_HALO = 32
HEAD_DIM = 128
FORGET_FLOOR = 1e-20
N_LEVELS = 7

VMEM_LIMIT = 56 * 1024 * 1024


def _cparams(sem):
    return pltpu.CompilerParams(dimension_semantics=sem, vmem_limit_bytes=VMEM_LIMIT)


def _rms(x, g):
    return x * lax.rsqrt(jnp.mean(x * x, axis=-1, keepdims=True) + EPS) * g


def _layer_norm(x, g, b):
    mu = jnp.mean(x, axis=-1, keepdims=True)
    xc = x - mu
    var = jnp.mean(xc * xc, axis=-1, keepdims=True)
    return xc * lax.rsqrt(var + EPS) * g + b


def _dot(a, b):
    return jnp.dot(a, b, preferred_element_type=F32)


def _dot_nt(a, b):
    return lax.dot_general(a, b, (((1,), (1,)), ((), ())), preferred_element_type=F32)


def _dot_tn(a, b):
    return lax.dot_general(a, b, (((0,), (0,)), ((), ())), preferred_element_type=F32)


def _mod_kernel(c_ref, w_ref, b_ref, o_ref):
    c = c_ref[...]
    cond = c * jax.nn.sigmoid(c)
    o_ref[0] = jnp.dot(cond, w_ref[0], preferred_element_type=F32,
                       precision=lax.Precision.HIGHEST) + b_ref[0]


def _modulation(c, ada_w, ada_b):
    B, D = c.shape
    L, _, N = ada_w.shape
    bp = 8
    tn = 1024
    cp = jnp.zeros((bp, D), F32).at[:B].set(c)
    out = pl.pallas_call(
        _mod_kernel,
        grid=(L, N // tn),
        in_specs=[
            pl.BlockSpec((bp, D), lambda l, j: (0, 0)),
            pl.BlockSpec((1, D, tn), lambda l, j: (l, 0, j)),
            pl.BlockSpec((1, 1, tn), lambda l, j: (l, 0, j)),
        ],
        out_specs=pl.BlockSpec((1, bp, tn), lambda l, j: (l, 0, j)),
        out_shape=jax.ShapeDtypeStruct((L, bp, N), F32),
        compiler_params=_cparams(("arbitrary", "arbitrary")),
        name="ada_mod",
    )(cp, ada_w, ada_b.reshape(L, 1, N))
    return out[:, :B].reshape(L, B, N_MOD, D)


IN_ROWS = 256


def _in_proj_kernel(x_ref, mod_ref, g_ref, w_ref, z_ref):
    tm = x_ref.shape[0]
    gain = g_ref[...] * (1.0 + mod_ref[0, 1:2, :])
    for r in range(tm // IN_ROWS):
        rows = slice(r * IN_ROWS, (r + 1) * IN_ROWS)
        x = x_ref[rows, :]
        h = x * lax.rsqrt(jnp.mean(x * x, axis=-1, keepdims=True) + EPS) * gain + mod_ref[0, 0:1, :]
        z_ref[rows, :] = _dot(h.astype(BF16), w_ref[0]).astype(z_ref.dtype)


def _in_proj(x2, mod, g, w, layer, T, tm=512):
    M, D = x2.shape
    N = w.shape[2]
    tpb = T // tm
    return pl.pallas_call(
        _in_proj_kernel,
        grid=(M // tm,),
        in_specs=[
            pl.BlockSpec((tm, D), lambda i: (i, 0)),
            pl.BlockSpec((1, N_MOD, D), lambda i: (i // tpb, 0, 0)),
            pl.BlockSpec((1, D), lambda i: (0, 0)),
            pl.BlockSpec((1, D, N), lambda i: (layer, 0, 0), pipeline_mode=pl.Buffered(1)),
        ],
        out_specs=pl.BlockSpec((tm, N), lambda i: (i, 0)),
        out_shape=jax.ShapeDtypeStruct((M, N), BF16),
        compiler_params=_cparams(("arbitrary",)),
        name="in_proj",
    )(x2, mod, g, w)


def _pool_kernel(x_ref, halo_ref, w_ref, s_ref, o_ref, buf):
    i = pl.program_id(1)
    tt = x_ref.shape[1]
    pc = w_ref.shape[1]
    buf[POOL_HALO:POOL_HALO + tt, :] = x_ref[0].astype(F32)
    buf[0:POOL_HALO, :] = jnp.where(i == 0, 0.0, halo_ref[0].astype(F32))
    t = i * tt + lax.broadcasted_iota(jnp.int32, (tt, pc), 0)
    for gi, win in enumerate(POOL_WINDOWS):
        cols = slice(gi * pc, (gi + 1) * pc)
        x = buf[POOL_HALO:POOL_HALO + tt, cols]
        acc = x
        for k in range(1, win):
            acc = acc + buf[POOL_HALO - k:POOL_HALO - k + tt, cols]
        cnt = jnp.minimum(t + 1, win).astype(F32)
        pooled = acc / cnt - x
        y = _dot(pooled.astype(BF16), w_ref[gi]) * s_ref[:, cols]
        o_ref[0, :, cols] = y.astype(o_ref.dtype)


def _pool_mixer(z3, w, scale, tt=1024):
    B, T, _ = z3.shape
    dg = scale.shape[-1]
    hb = tt // POOL_HALO
    return pl.pallas_call(
        _pool_kernel,
        grid=(B, T // tt),
        in_specs=[
            pl.BlockSpec((1, tt, dg), lambda b, i: (b, i, 0)),
            pl.BlockSpec((1, POOL_HALO, dg), lambda b, i: (b, jnp.maximum(i * hb - 1, 0), 0)),
            pl.BlockSpec(w.shape, lambda b, i: (0, 0, 0)),
            pl.BlockSpec((1, dg), lambda b, i: (0, 0)),
        ],
        out_specs=pl.BlockSpec((1, tt, dg), lambda b, i: (b, i, 0)),
        out_shape=jax.ShapeDtypeStruct((B, T, dg), BF16),
        scratch_shapes=[pltpu.VMEM((tt + POOL_HALO, dg), F32)],
        compiler_params=_cparams(("arbitrary", "arbitrary")),
        name="pool_mixer",
    )(z3, z3, w, scale)


def _sgu_kernel(u_ref, v_ref, g_ref, b_ref, w_ref, bst_ref, o_ref):
    tt = u_ref.shape[1]
    nh = w_ref.shape[0]
    hd = u_ref.shape[2] // nh
    vb = _layer_norm(v_ref[0].astype(F32), g_ref[...], b_ref[...]).astype(BF16)
    r = lax.broadcasted_iota(jnp.int32, (SGU_BLOCK, SGU_BLOCK), 0)
    c = lax.broadcasted_iota(jnp.int32, (SGU_BLOCK, SGU_BLOCK), 1)
    causal = (c // CHUNK) <= (r // CHUNK)
    for h in range(nh):
        cols = slice(h * hd, (h + 1) * hd)
        wm = jnp.where(causal, w_ref[h], 0.0).astype(BF16)
        bias = bst_ref[:, h:h + 1]
        for n in range(tt // SGU_BLOCK):
            rows = slice(n * SGU_BLOCK, (n + 1) * SGU_BLOCK)
            mixed = _dot(wm, vb[rows, cols]) + bias
            o_ref[0, rows, cols] = (u_ref[0, rows, cols].astype(F32) * mixed).astype(o_ref.dtype)


def _sgu_mixer(z3, norm_g, norm_b, ws, bs, tt=1024):
    B, T, _ = z3.shape
    dg = norm_g.shape[-1]
    nh = ws.shape[0]
    return pl.pallas_call(
        _sgu_kernel,
        grid=(B, T // tt),
        in_specs=[
            pl.BlockSpec((1, tt, dg), lambda b, i: (b, i, 1)),
            pl.BlockSpec((1, tt, dg), lambda b, i: (b, i, 2)),
            pl.BlockSpec((1, dg), lambda b, i: (0, 0)),
            pl.BlockSpec((1, dg), lambda b, i: (0, 0)),
            pl.BlockSpec(ws.shape, lambda b, i: (0, 0, 0)),
            pl.BlockSpec((SGU_BLOCK, nh), lambda b, i: (0, 0)),
        ],
        out_specs=pl.BlockSpec((1, tt, dg), lambda b, i: (b, i, 0)),
        out_shape=jax.ShapeDtypeStruct((B, T, dg), BF16),
        compiler_params=_cparams(("arbitrary", "arbitrary")),
        name="sgu_mixer",
    )(z3, z3, norm_g, norm_b, ws, bs.T)


CONV_ROWS = 64
SUBLANES = 8
LANES = 128


def _conv_kernel(a_ref, g_ref, ah_ref, gh_ref, dw_ref, dwb_ref, ng_ref, nb_ref, pw_ref, pwb_ref,
                 o_ref, buf, cbuf):
    i = pl.program_id(1)
    tt = a_ref.shape[1]
    buf[CONV_HALO:CONV_HALO + tt, :] = a_ref[0].astype(F32) * jax.nn.sigmoid(g_ref[0].astype(F32))
    buf[0:CONV_HALO, :] = jnp.where(i == 0, 0.0, ah_ref[0].astype(F32) * jax.nn.sigmoid(gh_ref[0].astype(F32)))
    for cg in range(buf.shape[1] // LANES):
        cols = slice(cg * LANES, (cg + 1) * LANES)
        for rb in range(tt // CONV_ROWS):
            r0 = rb * CONV_ROWS
            hblk = buf[r0:r0 + CONV_HALO + CONV_ROWS, cols]
            acc = jnp.broadcast_to(dwb_ref[:, cols], (CONV_ROWS, LANES))
            for r in range(SUBLANES):
                hr = hblk if r == 0 else pltpu.roll(hblk, r, axis=0)
                for a in range((CONV_WIDTH - 1 - r) // SUBLANES + 1):
                    s = SUBLANES * a + r
                    k = CONV_WIDTH - 1 - s
                    lo = CONV_HALO - SUBLANES * a
                    acc = acc + hr[lo:lo + CONV_ROWS, :] * dw_ref[k:k + 1, cols]
            cbuf[r0:r0 + CONV_ROWS, cols] = acc
    h = _layer_norm(cbuf[...], ng_ref[...], nb_ref[...])
    h = h * jax.nn.sigmoid(h)
    o_ref[0] = (_dot(h.astype(BF16), pw_ref[...]) + pwb_ref[...]).astype(o_ref.dtype)


def _conv_mixer(z3, dw, dw_b, ng, nb, pw, pw_b, tt=512):
    B, T, _ = z3.shape
    dg = dw.shape[-1]
    hb = tt // CONV_HALO
    halo_map = lambda col: (lambda b, i: (b, jnp.maximum(i * hb - 1, 0), col))
    vec = pl.BlockSpec((1, dg), lambda b, i: (0, 0))
    return pl.pallas_call(
        _conv_kernel,
        grid=(B, T // tt),
        in_specs=[
            pl.BlockSpec((1, tt, dg), lambda b, i: (b, i, 3)),
            pl.BlockSpec((1, tt, dg), lambda b, i: (b, i, 4)),
            pl.BlockSpec((1, CONV_HALO, dg), halo_map(3)),
            pl.BlockSpec((1, CONV_HALO, dg), halo_map(4)),
            pl.BlockSpec(dw.shape, lambda b, i: (0, 0)),
            vec, vec, vec,
            pl.BlockSpec(pw.shape, lambda b, i: (0, 0)),
            vec,
        ],
        out_specs=pl.BlockSpec((1, tt, dg), lambda b, i: (b, i, 0)),
        out_shape=jax.ShapeDtypeStruct((B, T, dg), BF16),
        scratch_shapes=[pltpu.VMEM((tt + CONV_HALO, dg), F32), pltpu.VMEM((tt, dg), F32)],
        compiler_params=_cparams(("arbitrary", "arbitrary")),
        name="conv_mixer",
    )(z3, z3, z3, z3, dw, dw_b, ng, nb, pw, pw_b)


def _hgrn_constants():
    C = CHUNK
    r = np.arange(C)
    blocks = [(r[None, :] <= r[:, None])]
    lvl = np.full((C, C), N_LEVELS, np.int32)
    lvl[r, r] = 0
    for l in range(1, N_LEVELS):
        s = C >> l
        p = r % (2 * s)
        m = r - p + s - 1
        upper = p >= s
        up = (r[None, :] > m[:, None]) & (r[None, :] <= r[:, None])
        lo = (r[None, :] > r[:, None]) & (r[None, :] <= m[:, None])
        if s < SUBLANES:
            blocks.append(np.where(upper[:, None], up, lo))
        same = (r[:, None] // (2 * s)) == (r[None, :] // (2 * s))
        lvl[same & upper[:, None] & (~upper)[None, :]] = l
    mat = np.concatenate(blocks, axis=0).astype(np.float32)
    return np.concatenate([mat, mat], axis=1), lvl


def _level_exponent(b, s):
    parts = []
    for base in range(0, CHUNK, 2 * s):
        ref = jnp.broadcast_to(b[base + s - 1:base + s], (s, b.shape[1]))
        parts.append(ref - b[base:base + s])
        parts.append(b[base + s:base + 2 * s] - ref)
    return jnp.concatenate(parts, axis=0)


def _hgrn_kernel(q_ref, f_ref, v_ref, og_ref, lg_ref, ng_ref, mc_ref, lvl_ref, o_ref, s_scr, o_scr,
                 *, layer):
    i = pl.program_id(2)
    tt = q_ref.shape[1]

    @pl.when(i == 0)
    def _():
        s_scr[...] = jnp.zeros_like(s_scr)

    depth = lg_ref.shape[0]
    rows = [lg_ref[m:m + 1, :] for m in range(depth)]
    mx = functools.reduce(jnp.maximum, rows)
    es = [jnp.exp(row - mx) for row in rows]
    den = functools.reduce(lambda a, b: a + b, es)
    sm = [e / den for e in es]
    lb = functools.reduce(lambda a, b: a + b, sm[:layer + 1]) - sm[0]

    fz = f_ref[0].astype(F32)
    sig = jax.nn.sigmoid(fz)
    f = lb + (1.0 - lb) * sig
    logf = jnp.log2(jnp.maximum(f, FORGET_FLOOR))
    kk = (1.0 - lb) * (1.0 - sig)
    lvl = lvl_ref[...]
    scale = HEAD_DIM ** -0.5
    nhead = q_ref.shape[2] // HEAD_DIM
    nchunk = tt // CHUNK
    n_fine = mc_ref.shape[0] // CHUNK - 1

    hi = logf.astype(BF16)
    lo = (logf - hi.astype(F32)).astype(BF16)
    e_all = []
    for c in range(nchunk):
        rows = slice(c * CHUNK, (c + 1) * CHUNK)
        e_all.append(_dot(mc_ref[...], jnp.concatenate([hi[rows], lo[rows]], axis=0)))

    units = [(c, h) for c in range(nchunk) for h in range(nhead)]
    rows_of = lambda c: slice(c * CHUNK, (c + 1) * CHUNK)
    cols_of = lambda h: slice(h * HEAD_DIM, (h + 1) * HEAD_DIM)
    level_products, qes, upds, decays = {}, {}, {}, {}
    for c, h in units:
        e_c = e_all[c][:, cols_of(h)]
        bcum = e_c[0:CHUNK]
        qc = q_ref[0, rows_of(c), cols_of(h)].astype(F32) * scale
        kc = kk[rows_of(c), cols_of(h)]
        qb, kb = qc.astype(BF16), kc.astype(BF16)
        prods = [jnp.sum(qc * kc, axis=-1, keepdims=True)]
        for l in range(1, N_LEVELS):
            s = CHUNK >> l
            if s >= SUBLANES:
                e_l = _level_exponent(bcum, s)
            else:
                fine = l - (N_LEVELS - n_fine)
                e_l = e_c[(fine + 1) * CHUNK:(fine + 2) * CHUNK]
            ex = jnp.exp2(e_l).astype(BF16)
            prods.append(_dot_nt(qb * ex, kb * ex))
        level_products[c, h] = prods
        qes[c, h] = (qc * jnp.exp2(bcum)).astype(BF16)
        b_last = bcum[CHUNK - 1:CHUNK, :]
        kd = (kc * jnp.exp2(b_last - bcum)).astype(BF16)
        upds[c, h] = _dot_tn(v_ref[0, rows_of(c), cols_of(h)], kd)
        decays[c, h] = jnp.exp2(b_last)
    is_level = [lvl == l for l in range(N_LEVELS)]
    for c, h in units:
        prods = level_products[c, h]
        att = jnp.where(is_level[0], prods[0], 0.0)
        for l in range(1, N_LEVELS):
            att = jnp.where(is_level[l], prods[l], att)
        o_scr[rows_of(c), cols_of(h)] = _dot(att.astype(BF16), v_ref[0, rows_of(c), cols_of(h)])
    for h in range(nhead):
        st = s_scr[h]
        for c in range(nchunk):
            o_scr[rows_of(c), cols_of(h)] += _dot_nt(qes[c, h], st.astype(BF16))
            st = st * decays[c, h] + upds[c, h]
        s_scr[h] = st

    og = og_ref[0].astype(F32)
    for h in range(nhead):
        cols = slice(h * HEAD_DIM, (h + 1) * HEAD_DIM)
        o = _rms(o_scr[:, cols], ng_ref[:, cols])
        o_ref[0, :, cols] = (o * (og[:, cols] * jax.nn.sigmoid(og[:, cols]))).astype(o_ref.dtype)


HGRN_HEADS_PER_STEP = 4


def _hgrn_mixer(z3, lb_logits, norm_g, layer, col0, tt=512):
    B, T, _ = z3.shape
    dg = norm_g.shape[-1]
    hw = HGRN_HEADS_PER_STEP * HEAD_DIM
    ng = dg // hw
    c0 = col0 // hw
    mat, lvl = _hgrn_constants()
    part = lambda p: pl.BlockSpec((1, tt, hw), lambda b, h, i: (b, i, c0 + p * ng + h))
    return pl.pallas_call(
        functools.partial(_hgrn_kernel, layer=layer),
        grid=(B, ng, T // tt),
        in_specs=[
            part(0), part(1), part(2), part(3),
            pl.BlockSpec((lb_logits.shape[0], hw), lambda b, h, i: (0, h)),
            pl.BlockSpec((1, hw), lambda b, h, i: (0, h)),
            pl.BlockSpec(mat.shape, lambda b, h, i: (0, 0)),
            pl.BlockSpec(lvl.shape, lambda b, h, i: (0, 0)),
        ],
        out_specs=pl.BlockSpec((1, tt, hw), lambda b, h, i: (b, i, h)),
        out_shape=jax.ShapeDtypeStruct((B, T, dg), BF16),
        scratch_shapes=[pltpu.VMEM((HGRN_HEADS_PER_STEP, HEAD_DIM, HEAD_DIM), F32),
                        pltpu.VMEM((tt, hw), F32)],
        compiler_params=_cparams(("arbitrary", "arbitrary", "arbitrary")),
        name="hgrn_mixer",
    )(z3, z3, z3, z3, lb_logits, norm_g, jnp.asarray(mat, BF16), jnp.asarray(lvl))


OUT_ROWS = 256


def _out_proj_kernel(ya_ref, yb_ref, yc_ref, yd_ref, w_ref, x_ref, mod_ref, gpost_ref, gpre_ref,
                     x1_ref, h2_ref, cat):
    tm = x_ref.shape[0]
    dg = ya_ref.shape[1]
    for p, ref in enumerate((ya_ref, yb_ref, yc_ref, yd_ref)):
        cat[:, p * dg:(p + 1) * dg] = ref[...]
    gain_post = mod_ref[0, 2:3, :] * gpost_ref[...]
    gain_pre = gpre_ref[...] * (1.0 + mod_ref[0, 4:5, :])
    for r in range(tm // OUT_ROWS):
        rows = slice(r * OUT_ROWS, (r + 1) * OUT_ROWS)
        y = _dot(cat[rows, :], w_ref[0])
        x1 = x_ref[rows, :] + y * lax.rsqrt(jnp.mean(y * y, axis=-1, keepdims=True) + EPS) * gain_post
        x1_ref[rows, :] = x1
        h2 = x1 * lax.rsqrt(jnp.mean(x1 * x1, axis=-1, keepdims=True) + EPS) * gain_pre + mod_ref[0, 3:4, :]
        h2_ref[rows, :] = h2.astype(h2_ref.dtype)


def _out_proj(ys, w, layer, x2, mod, g_post, g_pre, T, tm=512):
    M, D = x2.shape
    dg = ys[0].shape[1]
    tpb = T // tm
    yspec = pl.BlockSpec((tm, dg), lambda i: (i, 0))
    vec = pl.BlockSpec((1, D), lambda i: (0, 0))
    return pl.pallas_call(
        _out_proj_kernel,
        grid=(M // tm,),
        in_specs=[
            yspec, yspec, yspec, yspec,
            pl.BlockSpec((1,) + w.shape[1:], lambda i: (layer, 0, 0), pipeline_mode=pl.Buffered(1)),
            pl.BlockSpec((tm, D), lambda i: (i, 0)),
            pl.BlockSpec((1, N_MOD, D), lambda i: (i // tpb, 0, 0)),
            vec, vec,
        ],
        out_specs=[pl.BlockSpec((tm, D), lambda i: (i, 0)), pl.BlockSpec((tm, D), lambda i: (i, 0))],
        out_shape=[jax.ShapeDtypeStruct((M, D), F32), jax.ShapeDtypeStruct((M, D), BF16)],
        scratch_shapes=[pltpu.VMEM((tm, len(ys) * dg), BF16)],
        compiler_params=_cparams(("arbitrary",)),
        name="out_proj",
    )(*ys, w, x2, mod, g_post, g_pre)


def _mlp_kernel(h_ref, w1_ref, w2_ref, x1_ref, mod_ref, g_ref, o_ref):
    j = pl.program_id(1)

    @pl.when(j == 0)
    def _():
        o_ref[...] = jnp.zeros_like(o_ref)

    a = _dot(h_ref[...], w1_ref[0])
    a = jnp.square(jnp.maximum(a, 0.0)).astype(BF16)
    o_ref[...] += _dot(a, w2_ref[0])

    @pl.when(j == pl.num_programs(1) - 1)
    def _():
        gain = mod_ref[0, 5:6, :] * g_ref[...]
        y = o_ref[...]
        o_ref[...] = x1_ref[...] + y * lax.rsqrt(jnp.mean(y * y, axis=-1, keepdims=True) + EPS) * gain


def _mlp(h2, w1, w2, layer, x1, mod, g, T, tm=512, tf=1024):
    M, D = x1.shape
    FF = w1.shape[2]
    tpb = T // tm
    return pl.pallas_call(
        _mlp_kernel,
        grid=(M // tm, FF // tf),
        in_specs=[
            pl.BlockSpec((tm, D), lambda i, j: (i, 0)),
            pl.BlockSpec((1, D, tf), lambda i, j: (layer, 0, j)),
            pl.BlockSpec((1, tf, D), lambda i, j: (layer, j, 0)),
            pl.BlockSpec((tm, D), lambda i, j: (i, 0)),
            pl.BlockSpec((1, N_MOD, D), lambda i, j: (i // tpb, 0, 0)),
            pl.BlockSpec((1, D), lambda i, j: (0, 0)),
        ],
        out_specs=pl.BlockSpec((tm, D), lambda i, j: (i, 0)),
        out_shape=jax.ShapeDtypeStruct((M, D), F32),
        compiler_params=_cparams(("arbitrary", "arbitrary")),
        name="mlp",
    )(h2, w1, w2, x1, mod, g)


def kernel(x, c, ada_w, ada_b, norm_mix_pre, norm_mix_post, norm_mlp_pre, norm_mlp_post, w_in, pool_w, pool_scale, sgu_norm_g, sgu_norm_b, sgu_w, sgu_b, conv_dw, conv_dw_b, conv_norm_g, conv_norm_b, conv_pw, conv_pw_b, hgrn_lb_logits, hgrn_norm_g, w_out, mlp_w1, mlp_w2):
    B, T, D = x.shape
    L = ada_w.shape[0]
    dg = D // N_GROUPS
    row = lambda a: a.reshape(1, -1)

    mod_all = _modulation(c, ada_w, ada_b)
    w_in_b, w_out_b, w1_b, w2_b = (w.astype(BF16) for w in (w_in, w_out, mlp_w1, mlp_w2))
    x2 = x.reshape(B * T, D)
    for l in range(L):
        mod = mod_all[l]
        z = _in_proj(x2, mod, row(norm_mix_pre[l]), w_in_b, l, T)
        z3 = z.reshape(B, T, -1)
        ya = _pool_mixer(z3, pool_w[l].astype(BF16), row(pool_scale[l]))
        yb = _sgu_mixer(z3, row(sgu_norm_g[l]), row(sgu_norm_b[l]), sgu_w[l], sgu_b[l])
        yc = _conv_mixer(z3, conv_dw[l], row(conv_dw_b[l]), row(conv_norm_g[l]), row(conv_norm_b[l]),
                         conv_pw[l].astype(BF16), row(conv_pw_b[l]))
        yd = _hgrn_mixer(z3, hgrn_lb_logits, row(hgrn_norm_g[l]), l, 5 * dg)
        ys = [y.reshape(B * T, dg) for y in (ya, yb, yc, yd)]
        x1, h2 = _out_proj(ys, w_out_b, l, x2, mod, row(norm_mix_post[l]), row(norm_mlp_pre[l]), T)
        x2 = _mlp(h2, w1_b, w2_b, l, x1, mod, row(norm_mlp_post[l]), T)
    return x2.reshape(B, T, D)
```

```python
import functools

import numpy as np
import jax
import jax.numpy as jnp
from jax import lax
from jax.experimental import pallas as pl
from jax.experimental.pallas import tpu as pltpu

F32 = jnp.float32
BF16 = jnp.bfloat16

EPS = 1e-6
N_MOD = 6
N_GROUPS = 4
POOL_WINDOWS = (2, 4, 8, 16)
POOL_HALO = 16
SGU_BLOCK = 128
CHUNK = 64
CONV_WIDTH = 31
CONV_HALO = 32
HEAD_DIM = 128
FORGET_FLOOR = 1e-20
N_LEVELS = 7

VMEM_LIMIT = 56 * 1024 * 1024


def _cparams(sem):
    return pltpu.CompilerParams(dimension_semantics=sem, vmem_limit_bytes=VMEM_LIMIT)


def _rms(x, g):
    return x * lax.rsqrt(jnp.mean(x * x, axis=-1, keepdims=True) + EPS) * g


def _layer_norm(x, g, b):
    mu = jnp.mean(x, axis=-1, keepdims=True)
    xc = x - mu
    var = jnp.mean(xc * xc, axis=-1, keepdims=True)
    return xc * lax.rsqrt(var + EPS) * g + b


def _dot(a, b):
    return jnp.dot(a, b, preferred_element_type=F32)


def _dot_nt(a, b):
    return lax.dot_general(a, b, (((1,), (1,)), ((), ())), preferred_element_type=F32)


def _dot_tn(a, b):
    return lax.dot_general(a, b, (((0,), (0,)), ((), ())), preferred_element_type=F32)


def _mod_kernel(c_ref, w_ref, b_ref, o_ref):
    c = c_ref[...]
    cond = c * jax.nn.sigmoid(c)
    o_ref[0] = jnp.dot(cond, w_ref[0], preferred_element_type=F32,
                       precision=lax.Precision.HIGHEST) + b_ref[0]


def _modulation(c, ada_w, ada_b):
    B, D = c.shape
    L, _, N = ada_w.shape
    bp = 8
    tn = 1024
    cp = jnp.zeros((bp, D), F32).at[:B].set(c)
    out = pl.pallas_call(
        _mod_kernel,
        grid=(L, N // tn),
        in_specs=[
            pl.BlockSpec((bp, D), lambda l, j: (0, 0)),
            pl.BlockSpec((1, D, tn), lambda l, j: (l, 0, j)),
            pl.BlockSpec((1, 1, tn), lambda l, j: (l, 0, j)),
        ],
        out_specs=pl.BlockSpec((1, bp, tn), lambda l, j: (l, 0, j)),
        out_shape=jax.ShapeDtypeStruct((L, bp, N), F32),
        compiler_params=_cparams(("arbitrary", "arbitrary")),
        name="ada_mod",
    )(cp, ada_w, ada_b.reshape(L, 1, N))
    return out[:, :B].reshape(L, B, N_MOD, D)


IN_ROWS = 256


def _in_proj_kernel(x_ref, mod_ref, g_ref, w_ref, z_ref):
    tm = x_ref.shape[0]
    gain = g_ref[...] * (1.0 + mod_ref[0, 1:2, :])
    for r in range(tm // IN_ROWS):
        rows = slice(r * IN_ROWS, (r + 1) * IN_ROWS)
        x = x_ref[rows, :]
        h = x * lax.rsqrt(jnp.mean(x * x, axis=-1, keepdims=True) + EPS) * gain + mod_ref[0, 0:1, :]
        z_ref[rows, :] = _dot(h.astype(BF16), w_ref[0]).astype(z_ref.dtype)


def _in_proj(x2, mod, g, w, layer, T, tm=512):
    M, D = x2.shape
    N = w.shape[2]
    tpb = T // tm
    return pl.pallas_call(
        _in_proj_kernel,
        grid=(M // tm,),
        in_specs=[
            pl.BlockSpec((tm, D), lambda i: (i, 0)),
            pl.BlockSpec((1, N_MOD, D), lambda i: (i // tpb, 0, 0)),
            pl.BlockSpec((1, D), lambda i: (0, 0)),
            pl.BlockSpec((1, D, N), lambda i: (layer, 0, 0), pipeline_mode=pl.Buffered(1)),
        ],
        out_specs=pl.BlockSpec((tm, N), lambda i: (i, 0)),
        out_shape=jax.ShapeDtypeStruct((M, N), BF16),
        compiler_params=_cparams(("arbitrary",)),
        name="in_proj",
    )(x2, mod, g, w)


def _pool_kernel(x_ref, halo_ref, w_ref, s_ref, o_ref, buf):
    i = pl.program_id(1)
    tt = x_ref.shape[1]
    pc = w_ref.shape[1]
    buf[POOL_HALO:POOL_HALO + tt, :] = x_ref[0].astype(F32)
    buf[0:POOL_HALO, :] = jnp.where(i == 0, 0.0, halo_ref[0].astype(F32))
    t = i * tt + lax.broadcasted_iota(jnp.int32, (tt, pc), 0)
    for gi, win in enumerate(POOL_WINDOWS):
        cols = slice(gi * pc, (gi + 1) * pc)
        x = buf[POOL_HALO:POOL_HALO + tt, cols]
        acc = x
        for k in range(1, win):
            acc = acc + buf[POOL_HALO - k:POOL_HALO - k + tt, cols]
        cnt = jnp.minimum(t + 1, win).astype(F32)
        pooled = acc / cnt - x
        y = _dot(pooled.astype(BF16), w_ref[gi]) * s_ref[:, cols]
        o_ref[0, :, cols] = y.astype(o_ref.dtype)


def _sgu_kernel(u_ref, v_ref, g_ref, b_ref, w_ref, bst_ref, o_ref):
    tt = u_ref.shape[1]
    nh = w_ref.shape[0]
    hd = u_ref.shape[2] // nh
    vb = _layer_norm(v_ref[0].astype(F32), g_ref[...], b_ref[...]).astype(BF16)
    r = lax.broadcasted_iota(jnp.int32, (SGU_BLOCK, SGU_BLOCK), 0)
    c = lax.broadcasted_iota(jnp.int32, (SGU_BLOCK, SGU_BLOCK), 1)
    causal = (c // CHUNK) <= (r // CHUNK)
    for h in range(nh):
        cols = slice(h * hd, (h + 1) * hd)
        wm = jnp.where(causal, w_ref[h], 0.0).astype(BF16)
        bias = bst_ref[:, h:h + 1]
        for n in range(tt // SGU_BLOCK):
            rows = slice(n * SGU_BLOCK, (n + 1) * SGU_BLOCK)
            mixed = _dot(wm, vb[rows, cols]) + bias
            o_ref[0, rows, cols] = (u_ref[0, rows, cols].astype(F32) * mixed).astype(o_ref.dtype)


def _pool_sgu_kernel(x_ref, halo_ref, pw_ref, ps_ref, u_ref, v_ref, g_ref, b_ref, w_ref, bst_ref,
                     ya_ref, yb_ref, buf):
    _pool_kernel(x_ref, halo_ref, pw_ref, ps_ref, ya_ref, buf)
    _sgu_kernel(u_ref, v_ref, g_ref, b_ref, w_ref, bst_ref, yb_ref)


def _pool_sgu_mixer(z3, pool_w, pool_scale, norm_g, norm_b, ws, bs, tt=1024):
    B, T, _ = z3.shape
    dg = norm_g.shape[-1]
    nh = ws.shape[0]
    hb = tt // POOL_HALO
    col = lambda c: pl.BlockSpec((1, tt, dg), lambda b, i: (b, i, c))
    vec = pl.BlockSpec((1, dg), lambda b, i: (0, 0))
    out = jax.ShapeDtypeStruct((B, T, dg), BF16)
    return pl.pallas_call(
        _pool_sgu_kernel,
        grid=(B, T // tt),
        in_specs=[
            col(0),
            pl.BlockSpec((1, POOL_HALO, dg), lambda b, i: (b, jnp.maximum(i * hb - 1, 0), 0)),
            pl.BlockSpec(pool_w.shape, lambda b, i: (0, 0, 0)),
            vec,
            col(1), col(2), vec, vec,
            pl.BlockSpec(ws.shape, lambda b, i: (0, 0, 0)),
            pl.BlockSpec((SGU_BLOCK, nh), lambda b, i: (0, 0)),
        ],
        out_specs=[col(0), col(0)],
        out_shape=[out, out],
        scratch_shapes=[pltpu.VMEM((tt + POOL_HALO, dg), F32)],
        compiler_params=_cparams(("arbitrary", "arbitrary")),
        name="pool_sgu_mixer",
    )(z3, z3, pool_w, pool_scale, z3, z3, norm_g, norm_b, ws, bs.T)


CONV_ROWS = 64
SUBLANES = 8
LANES = 128


def _conv_kernel(a_ref, g_ref, ah_ref, gh_ref, dw_ref, dwb_ref, ng_ref, nb_ref, pw_ref, pwb_ref,
                 o_ref, buf, cbuf):
    i = pl.program_id(1)
    tt = a_ref.shape[1]
    buf[CONV_HALO:CONV_HALO + tt, :] = a_ref[0].astype(F32) * jax.nn.sigmoid(g_ref[0].astype(F32))
    buf[0:CONV_HALO, :] = jnp.where(i == 0, 0.0, ah_ref[0].astype(F32) * jax.nn.sigmoid(gh_ref[0].astype(F32)))
    for cg in range(buf.shape[1] // LANES):
        cols = slice(cg * LANES, (cg + 1) * LANES)
        for rb in range(tt // CONV_ROWS):
            r0 = rb * CONV_ROWS
            hblk = buf[r0:r0 + CONV_HALO + CONV_ROWS, cols]
            acc = jnp.broadcast_to(dwb_ref[:, cols], (CONV_ROWS, LANES))
            for r in range(SUBLANES):
                hr = hblk if r == 0 else pltpu.roll(hblk, r, axis=0)
                for a in range((CONV_WIDTH - 1 - r) // SUBLANES + 1):
                    s = SUBLANES * a + r
                    k = CONV_WIDTH - 1 - s
                    lo = CONV_HALO - SUBLANES * a
                    acc = acc + hr[lo:lo + CONV_ROWS, :] * dw_ref[k:k + 1, cols]
            cbuf[r0:r0 + CONV_ROWS, cols] = acc
    h = _layer_norm(cbuf[...], ng_ref[...], nb_ref[...])
    h = h * jax.nn.sigmoid(h)
    o_ref[0] = (_dot(h.astype(BF16), pw_ref[...]) + pwb_ref[...]).astype(o_ref.dtype)


def _conv_mixer(z3, dw, dw_b, ng, nb, pw, pw_b, tt=512):
    B, T, _ = z3.shape
    dg = dw.shape[-1]
    hb = tt // CONV_HALO
    halo_map = lambda col: (lambda b, i: (b, jnp.maximum(i * hb - 1, 0), col))
    vec = pl.BlockSpec((1, dg), lambda b, i: (0, 0))
    return pl.pallas_call(
        _conv_kernel,
        grid=(B, T // tt),
        in_specs=[
            pl.BlockSpec((1, tt, dg), lambda b, i: (b, i, 3)),
            pl.BlockSpec((1, tt, dg), lambda b, i: (b, i, 4)),
            pl.BlockSpec((1, CONV_HALO, dg), halo_map(3)),
            pl.BlockSpec((1, CONV_HALO, dg), halo_map(4)),
            pl.BlockSpec(dw.shape, lambda b, i: (0, 0)),
            vec, vec, vec,
            pl.BlockSpec(pw.shape, lambda b, i: (0, 0)),
            vec,
        ],
        out_specs=pl.BlockSpec((1, tt, dg), lambda b, i: (b, i, 0)),
        out_shape=jax.ShapeDtypeStruct((B, T, dg), BF16),
        scratch_shapes=[pltpu.VMEM((tt + CONV_HALO, dg), F32), pltpu.VMEM((tt, dg), F32)],
        compiler_params=_cparams(("arbitrary", "arbitrary")),
        name="conv_mixer",
    )(z3, z3, z3, z3, dw, dw_b, ng, nb, pw, pw_b)


def _hgrn_constants():
    C = CHUNK
    r = np.arange(C)
    blocks = [(r[None, :] <= r[:, None])]
    lvl = np.full((C, C), N_LEVELS, np.int32)
    lvl[r, r] = 0
    for l in range(1, N_LEVELS):
        s = C >> l
        p = r % (2 * s)
        m = r - p + s - 1
        upper = p >= s
        up = (r[None, :] > m[:, None]) & (r[None, :] <= r[:, None])
        lo = (r[None, :] > r[:, None]) & (r[None, :] <= m[:, None])
        if s < SUBLANES:
            blocks.append(np.where(upper[:, None], up, lo))
        same = (r[:, None] // (2 * s)) == (r[None, :] // (2 * s))
        lvl[same & upper[:, None] & (~upper)[None, :]] = l
    mat = np.concatenate(blocks, axis=0).astype(np.float32)
    return np.concatenate([mat, mat], axis=1), lvl


def _level_exponent(b, s):
    parts = []
    for base in range(0, CHUNK, 2 * s):
        ref = jnp.broadcast_to(b[base + s - 1:base + s], (s, b.shape[1]))
        parts.append(ref - b[base:base + s])
        parts.append(b[base + s:base + 2 * s] - ref)
    return jnp.concatenate(parts, axis=0)


def _hgrn_kernel(q_ref, f_ref, v_ref, og_ref, lg_ref, ng_ref, mc_ref, lvl_ref, o_ref, s_scr, o_scr,
                 *, layer):
    i = pl.program_id(2)
    tt = q_ref.shape[1]

    @pl.when(i == 0)
    def _():
        s_scr[...] = jnp.zeros_like(s_scr)

    depth = lg_ref.shape[0]
    rows = [lg_ref[m:m + 1, :] for m in range(depth)]
    mx = functools.reduce(jnp.maximum, rows)
    es = [jnp.exp(row - mx) for row in rows]
    den = functools.reduce(lambda a, b: a + b, es)
    sm = [e / den for e in es]
    lb = functools.reduce(lambda a, b: a + b, sm[:layer + 1]) - sm[0]

    fz = f_ref[0].astype(F32)
    sig = jax.nn.sigmoid(fz)
    f = lb + (1.0 - lb) * sig
    logf = jnp.log2(jnp.maximum(f, FORGET_FLOOR))
    kk = (1.0 - lb) * (1.0 - sig)
    lvl = lvl_ref[...]
    scale = HEAD_DIM ** -0.5
    nhead = q_ref.shape[2] // HEAD_DIM
    nchunk = tt // CHUNK
    n_fine = mc_ref.shape[0] // CHUNK - 1

    hi = logf.astype(BF16)
    lo = (logf - hi.astype(F32)).astype(BF16)
    e_all = []
    for c in range(nchunk):
        rows = slice(c * CHUNK, (c + 1) * CHUNK)
        e_all.append(_dot(mc_ref[...], jnp.concatenate([hi[rows], lo[rows]], axis=0)))

    units = [(c, h) for c in range(nchunk) for h in range(nhead)]
    rows_of = lambda c: slice(c * CHUNK, (c + 1) * CHUNK)
    cols_of = lambda h: slice(h * HEAD_DIM, (h + 1) * HEAD_DIM)
    level_products, qes, upds, decays = {}, {}, {}, {}
    for c, h in units:
        e_c = e_all[c][:, cols_of(h)]
        bcum = e_c[0:CHUNK]
        qc = q_ref[0, rows_of(c), cols_of(h)].astype(F32) * scale
        kc = kk[rows_of(c), cols_of(h)]
        qb, kb = qc.astype(BF16), kc.astype(BF16)
        prods = [jnp.sum(qc * kc, axis=-1, keepdims=True)]
        for l in range(1, N_LEVELS):
            s = CHUNK >> l
            if s >= SUBLANES:
                e_l = _level_exponent(bcum, s)
            else:
                fine = l - (N_LEVELS - n_fine)
                e_l = e_c[(fine + 1) * CHUNK:(fine + 2) * CHUNK]
            ex = jnp.exp2(e_l).astype(BF16)
            prods.append(_dot_nt(qb * ex, kb * ex))
        level_products[c, h] = prods
        qes[c, h] = (qc * jnp.exp2(bcum)).astype(BF16)
        b_last = bcum[CHUNK - 1:CHUNK, :]
        kd = (kc * jnp.exp2(b_last - bcum)).astype(BF16)
        upds[c, h] = _dot_tn(v_ref[0, rows_of(c), cols_of(h)], kd)
        decays[c, h] = jnp.exp2(b_last)
    is_level = [lvl == l for l in range(N_LEVELS)]
    for c, h in units:
        prods = level_products[c, h]
        att = jnp.where(is_level[0], prods[0], 0.0)
        for l in range(1, N_LEVELS):
            att = jnp.where(is_level[l], prods[l], att)
        o_scr[rows_of(c), cols_of(h)] = _dot(att.astype(BF16), v_ref[0, rows_of(c), cols_of(h)])
    for h in range(nhead):
        st = s_scr[h]
        for c in range(nchunk):
            o_scr[rows_of(c), cols_of(h)] += _dot_nt(qes[c, h], st.astype(BF16))
            st = st * decays[c, h] + upds[c, h]
        s_scr[h] = st

    og = og_ref[0].astype(F32)
    for h in range(nhead):
        cols = slice(h * HEAD_DIM, (h + 1) * HEAD_DIM)
        o = _rms(o_scr[:, cols], ng_ref[:, cols])
        o_ref[0, :, cols] = (o * (og[:, cols] * jax.nn.sigmoid(og[:, cols]))).astype(o_ref.dtype)


HGRN_HEADS_PER_STEP = 4


def _hgrn_mixer(z3, lb_logits, norm_g, layer, col0, tt=512):
    B, T, _ = z3.shape
    dg = norm_g.shape[-1]
    hw = HGRN_HEADS_PER_STEP * HEAD_DIM
    ng = dg // hw
    c0 = col0 // hw
    mat, lvl = _hgrn_constants()
    part = lambda p: pl.BlockSpec((1, tt, hw), lambda b, h, i: (b, i, c0 + p * ng + h))
    return pl.pallas_call(
        functools.partial(_hgrn_kernel, layer=layer),
        grid=(B, ng, T // tt),
        in_specs=[
            part(0), part(1), part(2), part(3),
            pl.BlockSpec((lb_logits.shape[0], hw), lambda b, h, i: (0, h)),
            pl.BlockSpec((1, hw), lambda b, h, i: (0, h)),
            pl.BlockSpec(mat.shape, lambda b, h, i: (0, 0)),
            pl.BlockSpec(lvl.shape, lambda b, h, i: (0, 0)),
        ],
        out_specs=pl.BlockSpec((1, tt, hw), lambda b, h, i: (b, i, h)),
        out_shape=jax.ShapeDtypeStruct((B, T, dg), BF16),
        scratch_shapes=[pltpu.VMEM((HGRN_HEADS_PER_STEP, HEAD_DIM, HEAD_DIM), F32),
                        pltpu.VMEM((tt, hw), F32)],
        compiler_params=_cparams(("arbitrary", "arbitrary", "arbitrary")),
        name="hgrn_mixer",
    )(z3, z3, z3, z3, lb_logits, norm_g, jnp.asarray(mat, BF16), jnp.asarray(lvl))


OUT_ROWS = 256


def _out_proj_kernel(ya_ref, yb_ref, yc_ref, yd_ref, w_ref, x_ref, mod_ref, gpost_ref, gpre_ref,
                     x1_ref, h2_ref, cat):
    tm = x_ref.shape[0]
    dg = ya_ref.shape[1]
    for p, ref in enumerate((ya_ref, yb_ref, yc_ref, yd_ref)):
        cat[:, p * dg:(p + 1) * dg] = ref[...]
    gain_post = mod_ref[0, 2:3, :] * gpost_ref[...]
    gain_pre = gpre_ref[...] * (1.0 + mod_ref[0, 4:5, :])
    for r in range(tm // OUT_ROWS):
        rows = slice(r * OUT_ROWS, (r + 1) * OUT_ROWS)
        y = _dot(cat[rows, :], w_ref[0])
        x1 = x_ref[rows, :] + y * lax.rsqrt(jnp.mean(y * y, axis=-1, keepdims=True) + EPS) * gain_post
        x1_ref[rows, :] = x1
        h2 = x1 * lax.rsqrt(jnp.mean(x1 * x1, axis=-1, keepdims=True) + EPS) * gain_pre + mod_ref[0, 3:4, :]
        h2_ref[rows, :] = h2.astype(h2_ref.dtype)


def _out_proj(ys, w, layer, x2, mod, g_post, g_pre, T, tm=512):
    M, D = x2.shape
    dg = ys[0].shape[1]
    tpb = T // tm
    yspec = pl.BlockSpec((tm, dg), lambda i: (i, 0))
    vec = pl.BlockSpec((1, D), lambda i: (0, 0))
    return pl.pallas_call(
        _out_proj_kernel,
        grid=(M // tm,),
        in_specs=[
            yspec, yspec, yspec, yspec,
            pl.BlockSpec((1,) + w.shape[1:], lambda i: (layer, 0, 0), pipeline_mode=pl.Buffered(1)),
            pl.BlockSpec((tm, D), lambda i: (i, 0)),
            pl.BlockSpec((1, N_MOD, D), lambda i: (i // tpb, 0, 0)),
            vec, vec,
        ],
        out_specs=[pl.BlockSpec((tm, D), lambda i: (i, 0)), pl.BlockSpec((tm, D), lambda i: (i, 0))],
        out_shape=[jax.ShapeDtypeStruct((M, D), F32), jax.ShapeDtypeStruct((M, D), BF16)],
        scratch_shapes=[pltpu.VMEM((tm, len(ys) * dg), BF16)],
        compiler_params=_cparams(("arbitrary",)),
        name="out_proj",
    )(*ys, w, x2, mod, g_post, g_pre)


def _mlp_kernel(h_ref, w1_ref, w2_ref, x1_ref, mod_ref, g_ref, o_ref):
    j = pl.program_id(1)

    @pl.when(j == 0)
    def _():
        o_ref[...] = jnp.zeros_like(o_ref)

    a = _dot(h_ref[...], w1_ref[0])
    a = jnp.square(jnp.maximum(a, 0.0)).astype(BF16)
    o_ref[...] += _dot(a, w2_ref[0])

    @pl.when(j == pl.num_programs(1) - 1)
    def _():
        gain = mod_ref[0, 5:6, :] * g_ref[...]
        y = o_ref[...]
        o_ref[...] = x1_ref[...] + y * lax.rsqrt(jnp.mean(y * y, axis=-1, keepdims=True) + EPS) * gain


def _mlp(h2, w1, w2, layer, x1, mod, g, T, tm=512, tf=1024):
    M, D = x1.shape
    FF = w1.shape[2]
    tpb = T // tm
    return pl.pallas_call(
        _mlp_kernel,
        grid=(M // tm, FF // tf),
        in_specs=[
            pl.BlockSpec((tm, D), lambda i, j: (i, 0)),
            pl.BlockSpec((1, D, tf), lambda i, j: (layer, 0, j)),
            pl.BlockSpec((1, tf, D), lambda i, j: (layer, j, 0)),
            pl.BlockSpec((tm, D), lambda i, j: (i, 0)),
            pl.BlockSpec((1, N_MOD, D), lambda i, j: (i // tpb, 0, 0)),
            pl.BlockSpec((1, D), lambda i, j: (0, 0)),
        ],
        out_specs=pl.BlockSpec((tm, D), lambda i, j: (i, 0)),
        out_shape=jax.ShapeDtypeStruct((M, D), F32),
        compiler_params=_cparams(("arbitrary", "arbitrary")),
        name="mlp",
    )(h2, w1, w2, x1, mod, g)


def kernel(x, c, ada_w, ada_b, norm_mix_pre, norm_mix_post, norm_mlp_pre, norm_mlp_post, w_in, pool_w, pool_scale, sgu_norm_g, sgu_norm_b, sgu_w, sgu_b, conv_dw, conv_dw_b, conv_norm_g, conv_norm_b, conv_pw, conv_pw_b, hgrn_lb_logits, hgrn_norm_g, w_out, mlp_w1, mlp_w2):
    B, T, D = x.shape
    L = ada_w.shape[0]
    dg = D // N_GROUPS
    row = lambda a: a.reshape(1, -1)

    mod_all = _modulation(c, ada_w, ada_b)
    w_in_b, w_out_b, w1_b, w2_b = (w.astype(BF16) for w in (w_in, w_out, mlp_w1, mlp_w2))
    x2 = x.reshape(B * T, D)
    for l in range(L):
        mod = mod_all[l]
        z = _in_proj(x2, mod, row(norm_mix_pre[l]), w_in_b, l, T)
        z3 = z.reshape(B, T, -1)
        ya, yb = _pool_sgu_mixer(z3, pool_w[l].astype(BF16), row(pool_scale[l]),
                                 row(sgu_norm_g[l]), row(sgu_norm_b[l]), sgu_w[l], sgu_b[l])
        yc = _conv_mixer(z3, conv_dw[l], row(conv_dw_b[l]), row(conv_norm_g[l]), row(conv_norm_b[l]),
                         conv_pw[l].astype(BF16), row(conv_pw_b[l]))
        yd = _hgrn_mixer(z3, hgrn_lb_logits, row(hgrn_norm_g[l]), l, 5 * dg)
        ys = [y.reshape(B * T, dg) for y in (ya, yb, yc, yd)]
        x1, h2 = _out_proj(ys, w_out_b, l, x2, mod, row(norm_mix_post[l]), row(norm_mlp_pre[l]), T)
        x2 = _mlp(h2, w1_b, w2_b, l, x1, mod, row(norm_mlp_post[l]), T)
    return x2.reshape(B, T, D)
```
